```python
import math
import jax, jax.numpy as jnp
from jax import lax
import numpy as np

D_MODEL = 1024
BATCH = 4
SEQ = 8192
DEPTH = 2

GLA_HEADS = 4
GLA_DK = 64
GLA_DV = 128
GLA_QK_DIM = GLA_HEADS * GLA_DK
GLA_V_DIM = GLA_HEADS * GLA_DV
GATE_RANK = 16
GATE_NORMALIZER = 16.0
CHUNK = 64
CONV_DIM = 512
CONV_WIDTH = 3
IN_SPLITS = (GLA_QK_DIM, GLA_QK_DIM, GLA_V_DIM, GLA_V_DIM, GATE_RANK, CONV_DIM, CONV_DIM, CONV_DIM)
IN_DIM = sum(IN_SPLITS)
MIX_DIM = GLA_V_DIM + CONV_DIM
FFN_HIDDEN = int(math.ceil(8 * D_MODEL / 3 / 256) * 256)
PLE_DIM = 256
NORM_EPS = 1e-6

kernel_name = "hybrid_gla_shortconv_parallel_heads"


def _rmsnorm(x, g):
    xf = x.astype(jnp.float32)
    xf = xf * lax.rsqrt(jnp.mean(xf * xf, axis=-1, keepdims=True) + NORM_EPS)
    return (xf * g.astype(jnp.float32)).astype(x.dtype)


def _gla_chunked(q, k, v, log_a):
    b_, s_, h_, dk = q.shape
    dv = v.shape[-1]
    n = s_ // CHUNK

    def to_chunks(t):
        return t.reshape(b_, n, CHUNK, h_, t.shape[-1]).transpose(1, 0, 3, 2, 4)

    causal = jnp.tril(jnp.ones((CHUNK, CHUNK), dtype=bool))[:, :, None]

    def step(state, inp):
        qc, kc, vc, gc = inp
        bcum = jnp.cumsum(gc, axis=2)
        diff = bcum[:, :, :, None, :] - bcum[:, :, None, :, :]
        decay = jnp.where(causal, jnp.exp(jnp.where(causal, diff, 0.0)), 0.0)
        scores = jnp.einsum('bhtd,bhsd,bhtsd->bhts', qc, kc, decay)
        o_intra = jnp.einsum('bhts,bhsv->bhtv', scores, vc)
        o_inter = jnp.einsum('bhtd,bhdv->bhtv', qc * jnp.exp(bcum), state)
        b_last = bcum[:, :, -1, :]
        k_dec = kc * jnp.exp(b_last[:, :, None, :] - bcum)
        new_state = jnp.exp(b_last)[..., None] * state + jnp.einsum('bhsd,bhsv->bhdv', k_dec, vc)
        return new_state, o_intra + o_inter

    s0 = jnp.zeros((b_, h_, dk, dv), jnp.float32)
    _, o = lax.scan(step, s0, (to_chunks(q), to_chunks(k), to_chunks(v), to_chunks(log_a)))
    return o.transpose(1, 0, 3, 2, 4).reshape(b_, s_, h_, dv)


def _causal_depthwise_conv(u, w):
    return lax.conv_general_dilated(
        u, w[:, None, :].astype(u.dtype), window_strides=(1,),
        padding=[(CONV_WIDTH - 1, 0)],
        dimension_numbers=('NWC', 'WIO', 'NWC'),
        feature_group_count=u.shape[-1])


def setup_inputs(seed: int = 0) -> dict:
    key = jax.random.key(seed)
    ks = jax.random.split(key, 20)
    nrm = lambda k, shape, scale: jax.random.normal(k, shape, jnp.float32) * scale
    gain = lambda k, shape: 1.0 + 0.05 * jax.random.normal(k, shape, jnp.float32)
    return {
        "x": nrm(ks[0], (BATCH, SEQ, D_MODEL), 1.0),
        "p": nrm(ks[1], (DEPTH, BATCH, SEQ, PLE_DIM), 1.0),
        "ln1_g": gain(ks[2], (DEPTH, D_MODEL)),
        "w_in": nrm(ks[3], (DEPTH, D_MODEL, IN_DIM), D_MODEL ** -0.5),
        "w_gate_up": nrm(ks[4], (DEPTH, GATE_RANK, GLA_QK_DIM), GATE_RANK ** -0.5),
        "b_gate": nrm(ks[5], (DEPTH, GLA_QK_DIM), 0.1),
        "conv_w": nrm(ks[6], (DEPTH, CONV_WIDTH, CONV_DIM), CONV_WIDTH ** -0.5),
        "gn_g": gain(ks[7], (DEPTH, GLA_DV)),
        "w_out": nrm(ks[8], (DEPTH, MIX_DIM, D_MODEL), MIX_DIM ** -0.5),
        "ln2_g": gain(ks[9], (DEPTH, D_MODEL)),
        "w_gate_upffn": nrm(ks[10], (DEPTH, D_MODEL, 2 * FFN_HIDDEN), D_MODEL ** -0.5),
        "w_down": nrm(ks[11], (DEPTH, FFN_HIDDEN, D_MODEL), FFN_HIDDEN ** -0.5),
        "ln3_g": gain(ks[12], (DEPTH, D_MODEL)),
        "w_ple_gate": nrm(ks[13], (DEPTH, D_MODEL, D_MODEL), D_MODEL ** -0.5),
        "w_ple_proj": nrm(ks[14], (DEPTH, PLE_DIM, D_MODEL), PLE_DIM ** -0.5),
        "lnf_g": gain(ks[15], (D_MODEL,)),
    }


def reference(x, p, ln1_g, w_in, w_gate_up, b_gate, conv_w, gn_g, w_out, ln2_g,
              w_gate_upffn, w_down, ln3_g, w_ple_gate, w_ple_proj, lnf_g):
    b_, s_, _ = x.shape
    offsets = list(np.cumsum(IN_SPLITS)[:-1])
    h = x
    for i in range(DEPTH):
        u = _rmsnorm(h, ln1_g[i])
        proj = u @ w_in[i]
        q, k, v, g, g_lr, cb, cc, cx = jnp.split(proj, offsets, axis=-1)

        gate_logits = (g_lr @ w_gate_up[i] + b_gate[i]).astype(jnp.float32)
        log_a = jax.nn.log_sigmoid(gate_logits) / GATE_NORMALIZER
        qh = q.astype(jnp.float32).reshape(b_, s_, GLA_HEADS, GLA_DK) * (GLA_DK ** -0.5)
        kh = k.astype(jnp.float32).reshape(b_, s_, GLA_HEADS, GLA_DK)
        vh = v.astype(jnp.float32).reshape(b_, s_, GLA_HEADS, GLA_DV)
        ah = log_a.reshape(b_, s_, GLA_HEADS, GLA_DK)
        o = _gla_chunked(qh, kh, vh, ah).astype(x.dtype)
        gh = g.reshape(b_, s_, GLA_HEADS, GLA_DV)
        o = _rmsnorm(o, gn_g[i]) * jax.nn.silu(gh)
        o = o.reshape(b_, s_, GLA_V_DIM)

        y = cb * _causal_depthwise_conv(cc * cx, conv_w[i])

        mix = jnp.concatenate([o, y], axis=-1) @ w_out[i]
        h = h + mix

        u2 = _rmsnorm(h, ln2_g[i])
        a, bb = jnp.split(u2 @ w_gate_upffn[i], 2, axis=-1)
        h = h + (jax.nn.silu(a) * bb) @ w_down[i]

        u3 = _rmsnorm(h, ln3_g[i])
        gate = jax.nn.sigmoid(u3 @ w_ple_gate[i])
        h = h + gate * (p[i] @ w_ple_proj[i])
    return _rmsnorm(h, lnf_g)
```

```python
import functools

import jax
import jax.numpy as jnp
from jax import lax
from jax.experimental import pallas as pl
from jax.experimental.pallas import tpu as pltpu

F32 = jnp.float32
BF16 = jnp.bfloat16

GLA_HEADS = 4
GLA_DK = 64
GLA_DV = 128
GLA_QK_DIM = GLA_HEADS * GLA_DK
GLA_V_DIM = GLA_HEADS * GLA_DV
GATE_RANK = 16
GATE_NORMALIZER = 16.0
CONV_DIM = 512
CONV_WIDTH = 3
NORM_EPS = 1e-6

LANES = 128
SUBLANES = 8
MXU_DIM = 256
VMEM_LIMIT_BYTES = 56 * 1024 * 1024

Q_OFF = 0
K_OFF = Q_OFF + GLA_QK_DIM
V_OFF = K_OFF + GLA_QK_DIM
G_OFF = V_OFF + GLA_V_DIM
CB_OFF = G_OFF + GLA_V_DIM
CC_OFF = CB_OFF + CONV_DIM
CX_OFF = CC_OFF + CONV_DIM
GLR_OFF = CX_OFF + CONV_DIM
GLR_PAD = LANES
IN_DIM_PADDED = GLR_OFF + GLR_PAD

MIXER_TOKENS = 256
GLA_CHUNK = 256
FFN_TOKENS = 256
FFN_COLS = 256


def _rmsnorm(x, g):
    ms = jnp.mean(x * x, axis=-1, keepdims=True)
    return x * lax.rsqrt(ms + NORM_EPS) * g


def _dot(a, b):
    return jnp.dot(a, b, preferred_element_type=F32)


def _mixer_body(h_ref, ln1_ref, win_ref, wup_ref, bg_ref, cw_ref, gn_ref, wout_ref,
                o_ref, st_ref, xs_ref, mix_ref):
    ts = h_ref.shape[1]
    c_len = GLA_CHUNK

    @pl.when(pl.program_id(1) == 0)
    def _reset_sequence_state():
        st_ref[...] = jnp.zeros_like(st_ref)
        xs_ref[0:SUBLANES, :] = jnp.zeros((SUBLANES, CONV_DIM), F32)

    h = h_ref[0]
    u = _rmsnorm(h, ln1_ref[...]).astype(BF16)

    def proj(off, width):
        return _dot(u, win_ref[:, off:off + width])

    cb = proj(CB_OFF, CONV_DIM)
    xin = proj(CC_OFF, CONV_DIM) * proj(CX_OFF, CONV_DIM)
    xs_ref[SUBLANES:SUBLANES + ts, :] = xin
    x1 = xs_ref[SUBLANES - 1:SUBLANES - 1 + ts, :]
    x2 = xs_ref[SUBLANES - 2:SUBLANES - 2 + ts, :]
    y = cb * (cw_ref[0:1, :] * x2 + cw_ref[1:2, :] * x1 + cw_ref[2:3, :] * xin)
    mix_ref[:, GLA_V_DIM:GLA_V_DIM + CONV_DIM] = y.astype(BF16)
    xs_ref[0:SUBLANES, :] = xs_ref[ts:ts + SUBLANES, :]

    q = proj(Q_OFF, GLA_QK_DIM) * (GLA_DK ** -0.5)
    k = proj(K_OFF, GLA_QK_DIM)
    v = proj(V_OFF, GLA_V_DIM)
    g = proj(G_OFF, GLA_V_DIM)
    glr = proj(GLR_OFF, GLR_PAD).astype(BF16)
    logits = _dot(glr, wup_ref[...]) + bg_ref[...]
    log_a = (jnp.minimum(logits, 0.0) - jnp.log1p(jnp.exp(-jnp.abs(logits)))) * (1.0 / GATE_NORMALIZER)

    row = lax.broadcasted_iota(jnp.int32, (c_len, c_len), 0)
    col = lax.broadcasted_iota(jnp.int32, (c_len, c_len), 1)
    causal = row >= col
    tri = jnp.where(causal, 1.0, 0.0).astype(BF16)
    eye_r = lax.broadcasted_iota(jnp.int32, (GLA_DK, GLA_DK), 0)
    eye_c = lax.broadcasted_iota(jnp.int32, (GLA_DK, GLA_DK), 1)
    eye = eye_r == eye_c

    for c in range(ts // c_len):
        rows = slice(c * c_len, (c + 1) * c_len)
        la = log_a[rows]
        la_hi = la.astype(BF16)
        la_lo = (la - la_hi.astype(F32)).astype(BF16)
        bc = _dot(tri, la_hi) + _dot(tri, la_lo)
        b_last = bc[c_len - 1:c_len, :]
        qe = q[rows] * jnp.exp(bc)
        ke = k[rows] * jnp.exp(-bc)
        kd = k[rows] * jnp.exp(b_last - bc)
        e_last = jnp.exp(b_last)
        for hd in range(GLA_HEADS):
            dk = slice(hd * GLA_DK, (hd + 1) * GLA_DK)
            dv = slice(hd * GLA_DV, (hd + 1) * GLA_DV)
            qh = qe[:, dk].astype(BF16)
            kh = ke[:, dk].astype(BF16)
            kdh = kd[:, dk].astype(BF16)
            vh = v[rows, dv].astype(BF16)
            sc = lax.dot_general(qh, kh, (((1,), (1,)), ((), ())), preferred_element_type=F32)
            sc = jnp.where(causal, sc, 0.0).astype(BF16)
            st = st_ref[hd]
            o = _dot(sc, vh) + _dot(qh, st.astype(BF16))
            e_col = jnp.sum(
                jnp.where(eye, jnp.broadcast_to(e_last[:, dk], (GLA_DK, GLA_DK)), 0.0),
                axis=1, keepdims=True)
            st_ref[hd] = st * e_col + lax.dot_general(
                kdh, vh, (((0,), (0,)), ((), ())), preferred_element_type=F32)
            gh = g[rows, dv]
            on = _rmsnorm(o, gn_ref[...]) * (gh * jax.nn.sigmoid(gh))
            mix_ref[rows, dv] = on.astype(BF16)

    o_ref[0] = h + _dot(mix_ref[...], wout_ref[...])


def _resident(shape):
    return pl.BlockSpec(shape, lambda *_: (0,) * len(shape), pipeline_mode=pl.Buffered(1))


def _mixer(h, ln1, w_in, w_up, b_gate, conv_w, gn, w_out):
    b, s, d = h.shape
    ts = MIXER_TOKENS
    return pl.pallas_call(
        _mixer_body,
        grid=(b, s // ts),
        in_specs=[
            pl.BlockSpec((1, ts, d), lambda i, j: (i, j, 0)),
            _resident(ln1.shape),
            _resident(w_in.shape),
            _resident(w_up.shape),
            _resident(b_gate.shape),
            _resident(conv_w.shape),
            _resident(gn.shape),
            _resident(w_out.shape),
        ],
        out_specs=pl.BlockSpec((1, ts, d), lambda i, j: (i, j, 0)),
        out_shape=jax.ShapeDtypeStruct(h.shape, h.dtype),
        scratch_shapes=[
            pltpu.VMEM((GLA_HEADS, GLA_DK, GLA_DV), F32),
            pltpu.VMEM((ts + SUBLANES, CONV_DIM), F32),
            pltpu.VMEM((ts, GLA_V_DIM + CONV_DIM), BF16),
        ],
        compiler_params=pltpu.CompilerParams(
            dimension_semantics=("arbitrary", "arbitrary"),
            vmem_limit_bytes=VMEM_LIMIT_BYTES),
        name="mixer",
    )(h, ln1, w_in, w_up, b_gate, conv_w, gn, w_out)


def _ffn_body(h_ref, p_ref, ln2_ref, wgu_ref, wdn_ref, ln3_ref, wpg_ref, wpp_ref, lnf_ref,
              o_ref, hid_ref, *, final):
    hidden = wdn_ref.shape[0]
    h = h_ref[...]
    u2 = _rmsnorm(h, ln2_ref[...]).astype(BF16)
    for c in range(hidden // FFN_COLS):
        lo = c * FFN_COLS
        a = _dot(u2, wgu_ref[:, lo:lo + FFN_COLS])
        bb = _dot(u2, wgu_ref[:, hidden + lo:hidden + lo + FFN_COLS])
        hid_ref[:, lo:lo + FFN_COLS] = (a * jax.nn.sigmoid(a) * bb).astype(BF16)
    h = h + _dot(hid_ref[...], wdn_ref[...])
    u3 = _rmsnorm(h, ln3_ref[...]).astype(BF16)
    gate = jax.nn.sigmoid(_dot(u3, wpg_ref[...]))
    h = h + gate * _dot(p_ref[...].astype(BF16), wpp_ref[...])
    if final:
        h = _rmsnorm(h, lnf_ref[...])
    o_ref[...] = h


def _ffn(h, p, ln2, w_gu, w_dn, ln3, w_pg, w_pp, lnf, *, final):
    t, d = h.shape
    tm = FFN_TOKENS
    hidden = w_dn.shape[0]
    return pl.pallas_call(
        functools.partial(_ffn_body, final=final),
        grid=(t // tm,),
        in_specs=[
            pl.BlockSpec((tm, d), lambda i: (i, 0)),
            pl.BlockSpec((tm, p.shape[1]), lambda i: (i, 0)),
            _resident(ln2.shape),
            _resident(w_gu.shape),
            _resident(w_dn.shape),
            _resident(ln3.shape),
            _resident(w_pg.shape),
            _resident(w_pp.shape),
            _resident(lnf.shape),
        ],
        out_specs=pl.BlockSpec((tm, d), lambda i: (i, 0)),
        out_shape=jax.ShapeDtypeStruct(h.shape, h.dtype),
        scratch_shapes=[pltpu.VMEM((tm, hidden), BF16)],
        compiler_params=pltpu.CompilerParams(
            dimension_semantics=("arbitrary",),
            vmem_limit_bytes=VMEM_LIMIT_BYTES),
        name="ffn",
    )(h, p, ln2, w_gu, w_dn, ln3, w_pg, w_pp, lnf)


def _reorder_w_in(w):
    glr_lo = 2 * GLA_QK_DIM + 2 * GLA_V_DIM
    glr_hi = glr_lo + GATE_RANK
    pad = jnp.zeros((w.shape[0], GLR_PAD - GATE_RANK), w.dtype)
    return jnp.concatenate([w[:, :glr_lo], w[:, glr_hi:], w[:, glr_lo:glr_hi], pad], axis=1)


def kernel(x, p, ln1_g, w_in, w_gate_up, b_gate, conv_w, gn_g, w_out, ln2_g,
           w_gate_upffn, w_down, ln3_g, w_ple_gate, w_ple_proj, lnf_g):
    b, s, d = x.shape
    depth = w_in.shape[0]
    assert s % MIXER_TOKENS == 0 and MIXER_TOKENS % GLA_CHUNK == 0
    assert (b * s) % FFN_TOKENS == 0 and w_down.shape[1] % FFN_COLS == 0
    assert w_in.shape[2] + GLR_PAD - GATE_RANK == IN_DIM_PADDED

    h = x
    for i in range(depth):
        w_in_i = _reorder_w_in(w_in[i]).astype(BF16)
        w_up_i = jnp.concatenate(
            [w_gate_up[i], jnp.zeros((GLR_PAD - GATE_RANK, GLA_QK_DIM), w_gate_up.dtype)],
            axis=0).astype(BF16)
        h = _mixer(h, ln1_g[i][None, :], w_in_i, w_up_i, b_gate[i][None, :], conv_w[i],
                   gn_g[i][None, :], w_out[i].astype(BF16))
        h = _ffn(h.reshape(b * s, d), p[i].reshape(b * s, -1), ln2_g[i][None, :],
                 w_gate_upffn[i].astype(BF16), w_down[i].astype(BF16), ln3_g[i][None, :],
                 w_ple_gate[i].astype(BF16), w_ple_proj[i].astype(BF16), lnf_g[None, :],
                 final=(i == depth - 1)).reshape(b, s, d)
    return h
```

```python
import functools

import jax
import jax.numpy as jnp
from jax import lax
from jax.experimental import pallas as pl
from jax.experimental.pallas import tpu as pltpu

F32 = jnp.float32
BF16 = jnp.bfloat16

GLA_HEADS = 4
GLA_DK = 64
GLA_DV = 128
GLA_QK_DIM = GLA_HEADS * GLA_DK
GLA_V_DIM = GLA_HEADS * GLA_DV
GATE_RANK = 16
GATE_NORMALIZER = 16.0
CONV_DIM = 512
CONV_WIDTH = 3
NORM_EPS = 1e-6

LANES = 128
SUBLANES = 8
VMEM_LIMIT_BYTES = 56 * 1024 * 1024

Q_OFF = 0
K_OFF = Q_OFF + GLA_QK_DIM
V_OFF = K_OFF + GLA_QK_DIM
G_OFF = V_OFF + GLA_V_DIM
TAIL_OFF = G_OFF + GLA_V_DIM
TAIL_DIM = -(-(GATE_RANK + 3 * CONV_DIM) // LANES) * LANES
IN_DIM_PADDED = TAIL_OFF + TAIL_DIM
CONV_WIN = CONV_DIM + LANES

MIXER_TOKENS = 512
GLA_CHUNK = 256
FFN_TOKENS = 512
FFN_COLS = 256

SAFE_DECAY_EXPONENT = 40.0


def _rmsnorm(x, g):
    ms = jnp.mean(x * x, axis=-1, keepdims=True)
    return x * lax.rsqrt(ms + NORM_EPS) * g


_dot = functools.partial(jnp.dot, preferred_element_type=F32)
_dot_nt = functools.partial(lax.dot_general, dimension_numbers=(((1,), (1,)), ((), ())),
                            preferred_element_type=F32)
_dot_tn = functools.partial(lax.dot_general, dimension_numbers=(((0,), (0,)), ((), ())),
                            preferred_element_type=F32)


def _head_dk(hd):
    return slice(hd * GLA_DK, (hd + 1) * GLA_DK)


def _head_dv(hd):
    return slice(hd * GLA_DV, (hd + 1) * GLA_DV)


def _mixer_body(h_ref, ln1_ref, win_ref, wup_ref, bg_ref, cw_ref, gn_ref, wout_ref,
                o_ref, st_ref, xs_ref, mix_ref,
                q_ref, k_ref, bc_ref, v_ref, gate_ref, ointer_ref, acc_ref):
    ts = h_ref.shape[1]
    c_len = GLA_CHUNK
    n_chunks = ts // c_len

    @pl.when(pl.program_id(1) == 0)
    def _reset_sequence_state():
        st_ref[...] = jnp.zeros_like(st_ref)
        xs_ref[0:SUBLANES, :] = jnp.zeros((SUBLANES, CONV_WIN), F32)

    u = _rmsnorm(h_ref[0], ln1_ref[...]).astype(BF16)

    def proj(off, width):
        return _dot(u, win_ref[:, off:off + width])

    t0 = proj(TAIL_OFF, LANES)
    q = proj(Q_OFF, GLA_QK_DIM) * (GLA_DK ** -0.5)
    logits = _dot(t0.astype(BF16), wup_ref[...]) + bg_ref[...]
    k = proj(K_OFF, GLA_QK_DIM)
    v = proj(V_OFF, GLA_V_DIM)
    log_a = (jnp.minimum(logits, 0.0) - jnp.log1p(jnp.exp(-jnp.abs(logits)))) * (1.0 / GATE_NORMALIZER)

    row = lax.broadcasted_iota(jnp.int32, (c_len, c_len), 0)
    col = lax.broadcasted_iota(jnp.int32, (c_len, c_len), 1)
    causal = row >= col
    tri = jnp.where(causal, 1.0, 0.0).astype(BF16)

    la_hi = log_a.astype(BF16)
    la_lo = (log_a - la_hi.astype(F32)).astype(BF16)
    bcs = []
    for c in range(n_chunks):
        rows = slice(c * c_len, (c + 1) * c_len)
        bcs.append(_dot(tri, la_hi[rows]) + _dot(tri, la_lo[rows]))
    g = proj(G_OFF, GLA_V_DIM)
    gate_ref[...] = g * jax.nn.sigmoid(g)

    worst = functools.reduce(jnp.maximum, [-bc[c_len - 1:c_len, :] for bc in bcs])
    need_exact = jnp.max(worst) > SAFE_DECAY_EXPONENT

    q_ref[...] = q
    k_ref[...] = k
    v_ref[...] = v
    vb = v.astype(BF16)
    qes, kes, kds, e_lasts = [], [], [], []
    for c in range(n_chunks):
        rows = slice(c * c_len, (c + 1) * c_len)
        bc = bcs[c]
        bc_ref[rows, :] = bc
        b_last = bc[c_len - 1:c_len, :]
        qes.append(q[rows] * jnp.exp(bc))
        kes.append(k[rows] * jnp.exp(-bc))
        kds.append(k[rows] * jnp.exp(b_last - bc))
        e_lasts.append(jnp.exp(b_last))

    scs = [[_dot_nt(qes[c][:, _head_dk(hd)].astype(BF16), kes[c][:, _head_dk(hd)].astype(BF16))
            for hd in range(GLA_HEADS)] for c in range(n_chunks)]

    eye = (lax.broadcasted_iota(jnp.int32, (GLA_DK, GLA_DK), 0)
           == lax.broadcasted_iota(jnp.int32, (GLA_DK, GLA_DK), 1))

    def chunk_rows(c):
        return slice(c * c_len, (c + 1) * c_len)

    def gla_chunk(c):
        rows = chunk_rows(c)
        outs = []
        for hd in range(GLA_HEADS):
            dk, dv = _head_dk(hd), _head_dv(hd)
            st = st_ref[hd]
            o_inter = _dot(qes[c][:, dk].astype(BF16), st.astype(BF16))
            ointer_ref[rows, dv] = o_inter
            sc = jnp.where(causal, scs[c][hd], 0.0).astype(BF16)
            outs.append(o_inter + _dot(sc, vb[rows, dv]))
            e_col = jnp.sum(
                jnp.where(eye, jnp.broadcast_to(e_lasts[c][:, dk], (GLA_DK, GLA_DK)), 0.0),
                axis=1, keepdims=True)
            st_ref[hd] = st * e_col + _dot_tn(kds[c][:, dk].astype(BF16), vb[rows, dv])
        return outs

    def finish_heads(c, outs):
        rows = chunk_rows(c)
        for hd, o in enumerate(outs):
            dv = _head_dv(hd)
            on = _rmsnorm(o, gn_ref[...]) * gate_ref[rows, dv]
            mix_ref[rows, dv] = on.astype(BF16)

    def project_rows(c, lo, hi):
        return _dot(mix_ref[chunk_rows(c), lo:hi], wout_ref[lo:hi, :])

    t1a = proj(TAIL_OFF + LANES, CONV_DIM)
    t1b = proj(TAIL_OFF + LANES + CONV_DIM, CONV_DIM)
    gla_outs = [gla_chunk(0)]
    t1c = proj(TAIL_OFF + LANES + 2 * CONV_DIM, CONV_DIM)
    gla_outs += [gla_chunk(c) for c in range(1, n_chunks)]

    cb = jnp.concatenate([t0, t1a], axis=1)
    xin = (jnp.concatenate([t1a[:, CONV_DIM - LANES:], t1b], axis=1)
           * jnp.concatenate([t1b[:, CONV_DIM - LANES:], t1c], axis=1))
    xs_ref[SUBLANES:SUBLANES + ts, :] = xin

    def conv_rows(c):
        rows = chunk_rows(c)
        lo = SUBLANES + c * c_len
        x1 = xs_ref[lo - 1:lo - 1 + c_len, :]
        x2 = xs_ref[lo - 2:lo - 2 + c_len, :]
        y = cb[rows] * (cw_ref[0:1, :] * x2 + cw_ref[1:2, :] * x1 + cw_ref[2:3, :] * xin[rows])
        y = pltpu.roll(y, CONV_WIN - GATE_RANK, 1)[:, 0:CONV_DIM]
        mix_ref[rows, GLA_V_DIM:GLA_V_DIM + CONV_DIM] = y.astype(BF16)

    def project_out(gla_parts):
        for c in range(n_chunks):
            rows = chunk_rows(c)
            o_ref[0, rows, :] = (h_ref[0, rows, :] + gla_parts[c]
                                 + project_rows(c, GLA_V_DIM, GLA_V_DIM + CONV_DIM))

    gla_parts = []
    for c in range(n_chunks):
        finish_heads(c, gla_outs[c])
        gla_parts.append(project_rows(c, 0, GLA_V_DIM))
    for c in range(n_chunks):
        conv_rows(c)
    xs_ref[0:SUBLANES, :] = xs_ref[ts:ts + SUBLANES, :]
    project_out(gla_parts)

    @pl.when(need_exact)
    def _redo_intra_chunk_exactly():
        expand = jnp.where(
            lax.broadcasted_iota(jnp.int32, (GLA_QK_DIM, GLA_V_DIM), 0) // GLA_DK
            == lax.broadcasted_iota(jnp.int32, (GLA_QK_DIM, GLA_V_DIM), 1) // GLA_DV,
            1.0, 0.0).astype(BF16)
        row_id = lax.broadcasted_iota(jnp.int32, (c_len, 1), 0)
        for c in range(n_chunks):
            base = c * c_len
            rows = slice(base, base + c_len)
            acc_ref[...] = jnp.zeros_like(acc_ref)

            def add_source_row(s, carry):
                k_s = k_ref[pl.ds(base + s, 1), :]
                b_s = bc_ref[pl.ds(base + s, 1), :]
                v_s = v_ref[pl.ds(base + s, 1), :]
                w = q_ref[rows, :] * k_s * jnp.exp(jnp.minimum(bc_ref[rows, :] - b_s, 0.0))
                r = _dot(w.astype(BF16), expand)
                acc_ref[...] += jnp.where(row_id >= s, r, 0.0) * v_s
                return carry

            lax.fori_loop(0, c_len, add_source_row, 0)
            finish_heads(c, [acc_ref[:, _head_dv(hd)] + ointer_ref[rows, _head_dv(hd)]
                             for hd in range(GLA_HEADS)])
        project_out([project_rows(c, 0, GLA_V_DIM) for c in range(n_chunks)])


def _layer_block(shape, layer):
    return pl.BlockSpec((None,) + tuple(shape[1:]),
                        lambda *_: (layer,) + (0,) * (len(shape) - 1),
                        pipeline_mode=pl.Buffered(1))


def _mixer(h, layer, ln1, w_in, w_up, b_gate, conv_w, gn, w_out):
    b, s, d = h.shape
    ts = MIXER_TOKENS
    return pl.pallas_call(
        _mixer_body,
        grid=(b, s // ts),
        in_specs=[
            pl.BlockSpec((1, ts, d), lambda i, j: (i, j, 0)),
            _layer_block(ln1.shape, layer),
            _layer_block(w_in.shape, layer),
            _layer_block(w_up.shape, layer),
            _layer_block(b_gate.shape, layer),
            _layer_block(conv_w.shape, layer),
            _layer_block(gn.shape, layer),
            _layer_block(w_out.shape, layer),
        ],
        out_specs=pl.BlockSpec((1, ts, d), lambda i, j: (i, j, 0)),
        out_shape=jax.ShapeDtypeStruct(h.shape, h.dtype),
        scratch_shapes=[
            pltpu.VMEM((GLA_HEADS, GLA_DK, GLA_DV), F32),
            pltpu.VMEM((ts + SUBLANES, CONV_WIN), F32),
            pltpu.VMEM((ts, GLA_V_DIM + CONV_DIM), BF16),
            pltpu.VMEM((ts, GLA_QK_DIM), F32),
            pltpu.VMEM((ts, GLA_QK_DIM), F32),
            pltpu.VMEM((ts, GLA_QK_DIM), F32),
            pltpu.VMEM((ts, GLA_V_DIM), F32),
            pltpu.VMEM((ts, GLA_V_DIM), F32),
            pltpu.VMEM((ts, GLA_V_DIM), F32),
            pltpu.VMEM((GLA_CHUNK, GLA_V_DIM), F32),
        ],
        compiler_params=pltpu.CompilerParams(
            dimension_semantics=("arbitrary", "arbitrary"),
            vmem_limit_bytes=VMEM_LIMIT_BYTES),
        name="mixer",
    )(h, ln1, w_in, w_up, b_gate, conv_w, gn, w_out)


def _ffn_body(h_ref, p_ref, ln2_ref, wgu_ref, wdn_ref, ln3_ref, wpg_ref, wpp_ref, lnf_ref,
              o_ref, hid_ref, *, final):
    hidden = wdn_ref.shape[0]
    h = h_ref[...]
    u2 = _rmsnorm(h, ln2_ref[...]).astype(BF16)
    for c in range(hidden // FFN_COLS):
        lo = c * FFN_COLS
        a = _dot(u2, wgu_ref[:, lo:lo + FFN_COLS])
        bb = _dot(u2, wgu_ref[:, hidden + lo:hidden + lo + FFN_COLS])
        hid_ref[:, lo:lo + FFN_COLS] = (a * jax.nn.sigmoid(a) * bb).astype(BF16)
    h = h + _dot(hid_ref[...], wdn_ref[...])
    u3 = _rmsnorm(h, ln3_ref[...]).astype(BF16)
    gate = jax.nn.sigmoid(_dot(u3, wpg_ref[...]))
    h = h + gate * _dot(p_ref[...].astype(BF16), wpp_ref[...])
    if final:
        h = _rmsnorm(h, lnf_ref[...])
    o_ref[...] = h


def _ffn(h, p, layer, ln2, w_gu, w_dn, ln3, w_pg, w_pp, lnf, *, final):
    t, d = h.shape
    tm = FFN_TOKENS
    hidden = w_dn.shape[1]
    return pl.pallas_call(
        functools.partial(_ffn_body, final=final),
        grid=(t // tm,),
        in_specs=[
            pl.BlockSpec((tm, d), lambda i: (i, 0)),
            pl.BlockSpec((None, tm, p.shape[2]), lambda i: (layer, i, 0)),
            _layer_block(ln2.shape, layer),
            _layer_block(w_gu.shape, layer),
            _layer_block(w_dn.shape, layer),
            _layer_block(ln3.shape, layer),
            _layer_block(w_pg.shape, layer),
            _layer_block(w_pp.shape, layer),
            pl.BlockSpec(lnf.shape, lambda i: (0, 0)),
        ],
        out_specs=pl.BlockSpec((tm, d), lambda i: (i, 0)),
        out_shape=jax.ShapeDtypeStruct(h.shape, h.dtype),
        scratch_shapes=[pltpu.VMEM((tm, hidden), BF16)],
        compiler_params=pltpu.CompilerParams(
            dimension_semantics=("arbitrary",),
            vmem_limit_bytes=VMEM_LIMIT_BYTES),
        name="ffn",
    )(h, p, ln2, w_gu, w_dn, ln3, w_pg, w_pp, lnf)


def kernel(x, p, ln1_g, w_in, w_gate_up, b_gate, conv_w, gn_g, w_out, ln2_g,
           w_gate_upffn, w_down, ln3_g, w_ple_gate, w_ple_proj, lnf_g):
    b, s, d = x.shape
    depth = w_in.shape[0]
    assert s % MIXER_TOKENS == 0 and MIXER_TOKENS % GLA_CHUNK == 0
    assert (b * s) % FFN_TOKENS == 0 and w_down.shape[1] % FFN_COLS == 0
    assert w_in.shape[2] == TAIL_OFF + GATE_RANK + CONV_WIDTH * CONV_DIM

    w_in_b = jnp.pad(w_in.astype(BF16), ((0, 0), (0, 0), (0, IN_DIM_PADDED - w_in.shape[2])))
    w_up_b = jnp.pad(w_gate_up.astype(BF16), ((0, 0), (0, LANES - GATE_RANK), (0, 0)))
    conv_w_p = jnp.pad(conv_w, ((0, 0), (0, 0), (GATE_RANK, CONV_WIN - CONV_DIM - GATE_RANK)))
    w_out_b = w_out.astype(BF16)
    w_gu_b = w_gate_upffn.astype(BF16)
    w_dn_b = w_down.astype(BF16)
    w_pg_b = w_ple_gate.astype(BF16)
    w_pp_b = w_ple_proj.astype(BF16)
    rowvec = lambda a: a[:, None, :]
    p_flat = p.reshape(depth, b * s, p.shape[-1])

    h = x
    for i in range(depth):
        h = _mixer(h, i, rowvec(ln1_g), w_in_b, w_up_b, rowvec(b_gate), conv_w_p,
                   rowvec(gn_g), w_out_b)
        h = _ffn(h.reshape(b * s, d), p_flat, i, rowvec(ln2_g), w_gu_b, w_dn_b, rowvec(ln3_g),
                 w_pg_b, w_pp_b, lnf_g[None, :], final=(i == depth - 1)).reshape(b, s, d)
    return h
```

```python
import functools

import jax
import jax.numpy as jnp
from jax import lax
from jax.experimental import pallas as pl
from jax.experimental.pallas import tpu as pltpu

F32 = jnp.float32
BF16 = jnp.bfloat16

GLA_HEADS = 4
GLA_DK = 64
GLA_DV = 128
GLA_QK_DIM = GLA_HEADS * GLA_DK
GLA_V_DIM = GLA_HEADS * GLA_DV
GATE_RANK = 16
GATE_NORMALIZER = 16.0
CONV_DIM = 512
CONV_WIDTH = 3
NORM_EPS = 1e-6

LANES = 128
SUBLANES = 8
VMEM_LIMIT_BYTES = 56 * 1024 * 1024

Q_OFF = 0
K_OFF = Q_OFF + GLA_QK_DIM
V_OFF = K_OFF + GLA_QK_DIM
G_OFF = V_OFF + GLA_V_DIM
TAIL_OFF = G_OFF + GLA_V_DIM
TAIL_DIM = -(-(GATE_RANK + 3 * CONV_DIM) // LANES) * LANES
IN_DIM_PADDED = TAIL_OFF + TAIL_DIM
CONV_WIN = CONV_DIM + LANES

MIXER_TOKENS = 512
GLA_CHUNK = 256
FFN_TOKENS = 1024
FFN_COLS = 256

SAFE_DECAY_EXPONENT = 40.0


def _rmsnorm(x, g):
    ms = jnp.mean(x * x, axis=-1, keepdims=True)
    return x * lax.rsqrt(ms + NORM_EPS) * g


_dot = functools.partial(jnp.dot, preferred_element_type=F32)
_dot_nt = functools.partial(lax.dot_general, dimension_numbers=(((1,), (1,)), ((), ())),
                            preferred_element_type=F32)
_dot_tn = functools.partial(lax.dot_general, dimension_numbers=(((0,), (0,)), ((), ())),
                            preferred_element_type=F32)


def _head_dk(hd):
    return slice(hd * GLA_DK, (hd + 1) * GLA_DK)


def _head_dv(hd):
    return slice(hd * GLA_DV, (hd + 1) * GLA_DV)


def _mixer_body(h_ref, ln1_ref, win_ref, wup_ref, bg_ref, cw_ref, gn_ref, wout_ref,
                o_ref, st_ref, xs_ref, mix_ref,
                q_ref, k_ref, bc_ref, v_ref, gate_ref, ointer_ref, acc_ref):
    ts = h_ref.shape[1]
    c_len = GLA_CHUNK
    n_chunks = ts // c_len

    @pl.when(pl.program_id(1) == 0)
    def _reset_sequence_state():
        st_ref[...] = jnp.zeros_like(st_ref)
        xs_ref[0:SUBLANES, :] = jnp.zeros((SUBLANES, CONV_WIN), F32)

    u = _rmsnorm(h_ref[0], ln1_ref[...]).astype(BF16)

    def proj(off, width):
        return _dot(u, win_ref[:, off:off + width])

    def chunk_rows(c):
        return slice(c * c_len, (c + 1) * c_len)


    t0 = proj(TAIL_OFF, LANES)
    q = proj(Q_OFF, GLA_QK_DIM) * (GLA_DK ** -0.5)
    logits = _dot(t0.astype(BF16), wup_ref[...]) + bg_ref[...]
    k = proj(K_OFF, GLA_QK_DIM)
    t1a = proj(TAIL_OFF + LANES, CONV_DIM)
    t1b = proj(TAIL_OFF + LANES + CONV_DIM, CONV_DIM)
    t1c = proj(TAIL_OFF + LANES + 2 * CONV_DIM, CONV_DIM)
    log_a = (jnp.minimum(logits, 0.0) - jnp.log1p(jnp.exp(-jnp.abs(logits)))) * (1.0 / GATE_NORMALIZER)

    cb = jnp.concatenate([t0, t1a], axis=1)
    xin = (jnp.concatenate([t1a[:, CONV_DIM - LANES:], t1b], axis=1)
           * jnp.concatenate([t1b[:, CONV_DIM - LANES:], t1c], axis=1))
    xs_ref[SUBLANES:SUBLANES + ts, :] = xin
    xe = xs_ref[...]
    x1e = pltpu.roll(xe, 1, 0)
    x2e = pltpu.roll(xe, 2, 0)
    for c in range(n_chunks):
        rows = chunk_rows(c)
        lo = SUBLANES + c * c_len
        x1 = x1e[lo:lo + c_len]
        x2 = x2e[lo:lo + c_len]
        y = cb[rows] * (cw_ref[0:1, :] * x2 + cw_ref[1:2, :] * x1 + cw_ref[2:3, :] * xin[rows])
        y = pltpu.roll(y, CONV_WIN - GATE_RANK, 1)[:, 0:CONV_DIM]
        mix_ref[rows, GLA_V_DIM:GLA_V_DIM + CONV_DIM] = y.astype(BF16)
    xs_ref[0:SUBLANES, :] = xs_ref[ts:ts + SUBLANES, :]

    row = lax.broadcasted_iota(jnp.int32, (c_len, c_len), 0)
    col = lax.broadcasted_iota(jnp.int32, (c_len, c_len), 1)
    causal = row >= col
    tri = jnp.where(causal, 1.0, 0.0).astype(BF16)

    la_hi = log_a.astype(BF16)
    la_lo = (log_a - la_hi.astype(F32)).astype(BF16)
    bcs = [_dot(tri, la_hi[chunk_rows(c)]) + _dot(tri, la_lo[chunk_rows(c)])
           for c in range(n_chunks)]
    v = proj(V_OFF, GLA_V_DIM)

    worst = functools.reduce(jnp.maximum, [-bc[c_len - 1:c_len, :] for bc in bcs])
    need_exact = jnp.max(worst) > SAFE_DECAY_EXPONENT

    q_ref[...] = q
    k_ref[...] = k
    v_ref[...] = v
    vb = v.astype(BF16)
    qes, kes, kds, e_lasts = [], [], [], []
    for c in range(n_chunks):
        rows = chunk_rows(c)
        bc = bcs[c]
        bc_ref[rows, :] = bc
        b_last = bc[c_len - 1:c_len, :]
        qes.append(q[rows] * jnp.exp(bc))
        kes.append(k[rows] * jnp.exp(-bc))
        kds.append(k[rows] * jnp.exp(b_last - bc))
        e_lasts.append(jnp.exp(b_last))

    scs = [[_dot_nt(qes[c][:, _head_dk(hd)].astype(BF16), kes[c][:, _head_dk(hd)].astype(BF16))
            for hd in range(GLA_HEADS)] for c in range(n_chunks)]
    g = proj(G_OFF, GLA_V_DIM)
    gate_ref[...] = g * jax.nn.sigmoid(g)

    eye = (lax.broadcasted_iota(jnp.int32, (GLA_DK, GLA_DK), 0)
           == lax.broadcasted_iota(jnp.int32, (GLA_DK, GLA_DK), 1))

    def gla_chunk(c):
        rows = chunk_rows(c)
        outs = []
        for hd in range(GLA_HEADS):
            dk, dv = _head_dk(hd), _head_dv(hd)
            st = st_ref[hd]
            o_inter = _dot(qes[c][:, dk].astype(BF16), st.astype(BF16))
            ointer_ref[rows, dv] = o_inter
            sc = jnp.where(causal, scs[c][hd], 0.0).astype(BF16)
            outs.append(o_inter + _dot(sc, vb[rows, dv]))
            e_col = jnp.sum(
                jnp.where(eye, jnp.broadcast_to(e_lasts[c][:, dk], (GLA_DK, GLA_DK)), 0.0),
                axis=1, keepdims=True)
            st_ref[hd] = st * e_col + _dot_tn(kds[c][:, dk].astype(BF16), vb[rows, dv])
        return outs

    def finish_heads(c, outs):
        rows = chunk_rows(c)
        for hd, o in enumerate(outs):
            dv = _head_dv(hd)
            on = _rmsnorm(o, gn_ref[...]) * gate_ref[rows, dv]
            mix_ref[rows, dv] = on.astype(BF16)

    def project_rows(c, lo, hi):
        return _dot(mix_ref[chunk_rows(c), lo:hi], wout_ref[lo:hi, :])

    def project_out(conv_parts):
        for c in range(n_chunks):
            rows = chunk_rows(c)
            o_ref[0, rows, :] = (h_ref[0, rows, :] + conv_parts[c]
                                 + project_rows(c, 0, GLA_V_DIM))

    gla_outs = [gla_chunk(c) for c in range(n_chunks)]
    conv_parts = [project_rows(c, GLA_V_DIM, GLA_V_DIM + CONV_DIM) for c in range(n_chunks)]
    for c in range(n_chunks):
        finish_heads(c, gla_outs[c])
    project_out(conv_parts)

    @pl.when(need_exact)
    def _redo_intra_chunk_exactly():
        expand = jnp.where(
            lax.broadcasted_iota(jnp.int32, (GLA_QK_DIM, GLA_V_DIM), 0) // GLA_DK
            == lax.broadcasted_iota(jnp.int32, (GLA_QK_DIM, GLA_V_DIM), 1) // GLA_DV,
            1.0, 0.0).astype(BF16)
        row_id = lax.broadcasted_iota(jnp.int32, (c_len, 1), 0)
        for c in range(n_chunks):
            base = c * c_len
            rows = slice(base, base + c_len)
            acc_ref[...] = jnp.zeros_like(acc_ref)

            def add_source_row(s, carry):
                k_s = k_ref[pl.ds(base + s, 1), :]
                b_s = bc_ref[pl.ds(base + s, 1), :]
                v_s = v_ref[pl.ds(base + s, 1), :]
                w = q_ref[rows, :] * k_s * jnp.exp(jnp.minimum(bc_ref[rows, :] - b_s, 0.0))
                r = _dot(w.astype(BF16), expand)
                acc_ref[...] += jnp.where(row_id >= s, r, 0.0) * v_s
                return carry

            lax.fori_loop(0, c_len, add_source_row, 0)
            finish_heads(c, [acc_ref[:, _head_dv(hd)] + ointer_ref[rows, _head_dv(hd)]
                             for hd in range(GLA_HEADS)])
        project_out([project_rows(c, GLA_V_DIM, GLA_V_DIM + CONV_DIM) for c in range(n_chunks)])


def _layer_block(shape, layer):
    return pl.BlockSpec((None,) + tuple(shape[1:]),
                        lambda *_: (layer,) + (0,) * (len(shape) - 1),
                        pipeline_mode=pl.Buffered(1))


def _mixer(h, layer, ln1, w_in, w_up, b_gate, conv_w, gn, w_out):
    b, s, d = h.shape
    ts = MIXER_TOKENS
    return pl.pallas_call(
        _mixer_body,
        grid=(b, s // ts),
        in_specs=[
            pl.BlockSpec((1, ts, d), lambda i, j: (i, j, 0)),
            _layer_block(ln1.shape, layer),
            _layer_block(w_in.shape, layer),
            _layer_block(w_up.shape, layer),
            _layer_block(b_gate.shape, layer),
            _layer_block(conv_w.shape, layer),
            _layer_block(gn.shape, layer),
            _layer_block(w_out.shape, layer),
        ],
        out_specs=pl.BlockSpec((1, ts, d), lambda i, j: (i, j, 0)),
        out_shape=jax.ShapeDtypeStruct(h.shape, h.dtype),
        scratch_shapes=[
            pltpu.VMEM((GLA_HEADS, GLA_DK, GLA_DV), F32),
            pltpu.VMEM((ts + SUBLANES, CONV_WIN), F32),
            pltpu.VMEM((ts, GLA_V_DIM + CONV_DIM), BF16),
            pltpu.VMEM((ts, GLA_QK_DIM), F32),
            pltpu.VMEM((ts, GLA_QK_DIM), F32),
            pltpu.VMEM((ts, GLA_QK_DIM), F32),
            pltpu.VMEM((ts, GLA_V_DIM), F32),
            pltpu.VMEM((ts, GLA_V_DIM), F32),
            pltpu.VMEM((ts, GLA_V_DIM), F32),
            pltpu.VMEM((GLA_CHUNK, GLA_V_DIM), F32),
        ],
        compiler_params=pltpu.CompilerParams(
            dimension_semantics=("arbitrary", "arbitrary"),
            vmem_limit_bytes=VMEM_LIMIT_BYTES),
        name="mixer",
    )(h, ln1, w_in, w_up, b_gate, conv_w, gn, w_out)


def _ffn_body(h_ref, p_ref, ln2_ref, wgu_ref, wdn_ref, ln3_ref, wpg_ref, wpp_ref, lnf_ref,
              o_ref, hid_ref, *, final):
    hidden = wdn_ref.shape[0]
    tm = h_ref.shape[0]
    o_ref[...] = _dot(p_ref[...].astype(BF16), wpp_ref[...])
    u2 = _rmsnorm(h_ref[...], ln2_ref[...]).astype(BF16)
    for c in range(hidden // FFN_COLS):
        lo = c * FFN_COLS
        a = _dot(u2, wgu_ref[:, lo:lo + FFN_COLS])
        bb = _dot(u2, wgu_ref[:, hidden + lo:hidden + lo + FFN_COLS])
        hid_ref[:, lo:lo + FFN_COLS] = (a * jax.nn.sigmoid(a) * bb).astype(BF16)
    halves = (slice(0, tm // 2), slice(tm // 2, tm))
    hs = [h_ref[r, :] + _dot(hid_ref[r, :], wdn_ref[...]) for r in halves]
    gates = [jax.nn.sigmoid(_dot(_rmsnorm(hr, ln3_ref[...]).astype(BF16), wpg_ref[...]))
             for hr in hs]
    for r, hr, gate in zip(halves, hs, gates):
        hr = hr + gate * o_ref[r, :]
        if final:
            hr = _rmsnorm(hr, lnf_ref[...])
        o_ref[r, :] = hr


def _ffn(h, p, layer, ln2, w_gu, w_dn, ln3, w_pg, w_pp, lnf, *, final):
    t, d = h.shape
    tm = FFN_TOKENS
    hidden = w_dn.shape[1]
    return pl.pallas_call(
        functools.partial(_ffn_body, final=final),
        grid=(t // tm,),
        in_specs=[
            pl.BlockSpec((tm, d), lambda i: (i, 0)),
            pl.BlockSpec((None, tm, p.shape[2]), lambda i: (layer, i, 0)),
            _layer_block(ln2.shape, layer),
            _layer_block(w_gu.shape, layer),
            _layer_block(w_dn.shape, layer),
            _layer_block(ln3.shape, layer),
            _layer_block(w_pg.shape, layer),
            _layer_block(w_pp.shape, layer),
            pl.BlockSpec(lnf.shape, lambda i: (0, 0)),
        ],
        out_specs=pl.BlockSpec((tm, d), lambda i: (i, 0)),
        out_shape=jax.ShapeDtypeStruct(h.shape, h.dtype),
        scratch_shapes=[pltpu.VMEM((tm, hidden), BF16)],
        compiler_params=pltpu.CompilerParams(
            dimension_semantics=("arbitrary",),
            vmem_limit_bytes=VMEM_LIMIT_BYTES),
        name="ffn",
    )(h, p, ln2, w_gu, w_dn, ln3, w_pg, w_pp, lnf)


def kernel(x, p, ln1_g, w_in, w_gate_up, b_gate, conv_w, gn_g, w_out, ln2_g,
           w_gate_upffn, w_down, ln3_g, w_ple_gate, w_ple_proj, lnf_g):
    b, s, d = x.shape
    depth = w_in.shape[0]
    assert s % MIXER_TOKENS == 0 and MIXER_TOKENS % GLA_CHUNK == 0
    assert (b * s) % FFN_TOKENS == 0 and w_down.shape[1] % FFN_COLS == 0
    assert w_in.shape[2] == TAIL_OFF + GATE_RANK + CONV_WIDTH * CONV_DIM

    w_in_b = jnp.pad(w_in.astype(BF16), ((0, 0), (0, 0), (0, IN_DIM_PADDED - w_in.shape[2])))
    w_up_b = jnp.pad(w_gate_up.astype(BF16), ((0, 0), (0, LANES - GATE_RANK), (0, 0)))
    conv_w_p = jnp.pad(conv_w, ((0, 0), (0, 0), (GATE_RANK, CONV_WIN - CONV_DIM - GATE_RANK)))
    w_out_b = w_out.astype(BF16)
    w_gu_b = w_gate_upffn.astype(BF16)
    w_dn_b = w_down.astype(BF16)
    w_pg_b = w_ple_gate.astype(BF16)
    w_pp_b = w_ple_proj.astype(BF16)
    rowvec = lambda a: a[:, None, :]
    p_flat = p.reshape(depth, b * s, p.shape[-1])

    h = x
    for i in range(depth):
        h = _mixer(h, i, rowvec(ln1_g), w_in_b, w_up_b, rowvec(b_gate), conv_w_p,
                   rowvec(gn_g), w_out_b)
        h = _ffn(h.reshape(b * s, d), p_flat, i, rowvec(ln2_g), w_gu_b, w_dn_b, rowvec(ln3_g),
                 w_pg_b, w_pp_b, lnf_g[None, :], final=(i == depth - 1)).reshape(b, s, d)
    return h
```

```python
import functools

import jax
import jax.numpy as jnp
from jax import lax
from jax.experimental import pallas as pl
from jax.experimental.pallas import tpu as pltpu

F32 = jnp.float32
BF16 = jnp.bfloat16

GLA_HEADS = 4
GLA_DK = 64
GLA_DV = 128
GLA_QK_DIM = GLA_HEADS * GLA_DK
GLA_V_DIM = GLA_HEADS * GLA_DV
GATE_RANK = 16
GATE_NORMALIZER = 16.0
CONV_DIM = 512
CONV_WIDTH = 3
NORM_EPS = 1e-6

LANES = 128
SUBLANES = 8
VMEM_LIMIT_BYTES = 56 * 1024 * 1024

Q_OFF = 0
K_OFF = Q_OFF + GLA_QK_DIM
V_OFF = K_OFF + GLA_QK_DIM
G_OFF = V_OFF + GLA_V_DIM
TAIL_OFF = G_OFF + GLA_V_DIM
TAIL_DIM = -(-(GATE_RANK + 3 * CONV_DIM) // LANES) * LANES
IN_DIM_PADDED = TAIL_OFF + TAIL_DIM
CONV_WIN = CONV_DIM + LANES

MIXER_TOKENS = 1024
MIXER_SUB_TOKENS = 512
GLA_CHUNK = 256
FFN_TOKENS = 1024
FFN_COLS = 256

SAFE_DECAY_EXPONENT = 40.0


def _rmsnorm(x, g):
    ms = jnp.mean(x * x, axis=-1, keepdims=True)
    return x * lax.rsqrt(ms + NORM_EPS) * g


_dot = functools.partial(jnp.dot, preferred_element_type=F32)
_dot_nt = functools.partial(lax.dot_general, dimension_numbers=(((1,), (1,)), ((), ())),
                            preferred_element_type=F32)
_dot_tn = functools.partial(lax.dot_general, dimension_numbers=(((0,), (0,)), ((), ())),
                            preferred_element_type=F32)


def _head_dk(hd):
    return slice(hd * GLA_DK, (hd + 1) * GLA_DK)


def _head_dv(hd):
    return slice(hd * GLA_DV, (hd + 1) * GLA_DV)


def _mixer_body(h_ref, ln1_ref, win_ref, wup_ref, bg_ref, cw_ref, gn_ref, wout_ref,
                o_ref, st_ref, xs_ref, mix_ref,
                q_ref, k_ref, bc_ref, v_ref, gate_ref, ointer_ref, acc_ref):
    ts = h_ref.shape[1]
    c_len = GLA_CHUNK
    n_chunks = ts // c_len

    @pl.when(pl.program_id(1) == 0)
    def _reset_sequence_state():
        st_ref[...] = jnp.zeros_like(st_ref)
        xs_ref[0:SUBLANES, :] = jnp.zeros((SUBLANES, CONV_WIN), F32)

    def chunk_rows(c):
        return slice(c * c_len, (c + 1) * c_len)

    row = lax.broadcasted_iota(jnp.int32, (c_len, c_len), 0)
    col = lax.broadcasted_iota(jnp.int32, (c_len, c_len), 1)
    causal = row >= col
    tri = jnp.where(causal, 1.0, 0.0).astype(BF16)
    eye = (lax.broadcasted_iota(jnp.int32, (GLA_DK, GLA_DK), 0)
           == lax.broadcasted_iota(jnp.int32, (GLA_DK, GLA_DK), 1))

    def finish_heads(c, outs):
        rows = chunk_rows(c)
        for hd, o in enumerate(outs):
            dv = _head_dv(hd)
            on = _rmsnorm(o, gn_ref[...]) * gate_ref[rows, dv]
            mix_ref[rows, dv] = on.astype(BF16)

    def project_rows(c, lo, hi):
        return _dot(mix_ref[chunk_rows(c), lo:hi], wout_ref[lo:hi, :])

    def project_out(chunks, conv_parts):
        for c, conv_part in zip(chunks, conv_parts):
            rows = chunk_rows(c)
            o_ref[0, rows, :] = (h_ref[0, rows, :] + conv_part
                                 + project_rows(c, 0, GLA_V_DIM))

    def sub_tile(r0, chunks):
        sub = len(chunks) * c_len
        srows = slice(r0, r0 + sub)
        u = _rmsnorm(h_ref[0, srows, :], ln1_ref[...]).astype(BF16)

        def proj(off, width):
            return _dot(u, win_ref[:, off:off + width])

        def local(c):
            return slice(c * c_len - r0, (c + 1) * c_len - r0)

        t0 = proj(TAIL_OFF, LANES)
        q = proj(Q_OFF, GLA_QK_DIM) * (GLA_DK ** -0.5)
        logits = _dot(t0.astype(BF16), wup_ref[...]) + bg_ref[...]
        k = proj(K_OFF, GLA_QK_DIM)
        t1a = proj(TAIL_OFF + LANES, CONV_DIM)
        t1b = proj(TAIL_OFF + LANES + CONV_DIM, CONV_DIM)
        t1c = proj(TAIL_OFF + LANES + 2 * CONV_DIM, CONV_DIM)
        log_a = ((jnp.minimum(logits, 0.0) - jnp.log1p(jnp.exp(-jnp.abs(logits))))
                 * (1.0 / GATE_NORMALIZER))

        cb = jnp.concatenate([t0, t1a], axis=1)
        xin = (jnp.concatenate([t1a[:, CONV_DIM - LANES:], t1b], axis=1)
               * jnp.concatenate([t1b[:, CONV_DIM - LANES:], t1c], axis=1))
        xs_ref[SUBLANES + r0:SUBLANES + r0 + sub, :] = xin
        xe = xs_ref[r0:r0 + sub + SUBLANES, :]
        x1 = pltpu.roll(xe, 1, 0)[SUBLANES:, :]
        x2 = pltpu.roll(xe, 2, 0)[SUBLANES:, :]
        y = cb * (cw_ref[0:1, :] * x2 + cw_ref[1:2, :] * x1 + cw_ref[2:3, :] * xin)
        y = pltpu.roll(y, CONV_WIN - GATE_RANK, 1)[:, 0:CONV_DIM]
        mix_ref[srows, GLA_V_DIM:GLA_V_DIM + CONV_DIM] = y.astype(BF16)

        la_hi = log_a.astype(BF16)
        la_lo = (log_a - la_hi.astype(F32)).astype(BF16)
        bcs = [_dot(tri, la_hi[local(c)]) + _dot(tri, la_lo[local(c)]) for c in chunks]
        v = proj(V_OFF, GLA_V_DIM)

        q_ref[srows, :] = q
        k_ref[srows, :] = k
        v_ref[srows, :] = v
        vb = v.astype(BF16)
        qes, kes, kds, e_lasts = [], [], [], []
        for c, bc in zip(chunks, bcs):
            bc_ref[chunk_rows(c), :] = bc
            b_last = bc[c_len - 1:c_len, :]
            qes.append(q[local(c)] * jnp.exp(bc))
            kes.append(k[local(c)] * jnp.exp(-bc))
            kds.append(k[local(c)] * jnp.exp(b_last - bc))
            e_lasts.append(jnp.exp(b_last))

        scs = [[_dot_nt(qe[:, _head_dk(hd)].astype(BF16), ke[:, _head_dk(hd)].astype(BF16))
                for hd in range(GLA_HEADS)] for qe, ke in zip(qes, kes)]
        g = proj(G_OFF, GLA_V_DIM)
        gate_ref[srows, :] = g * jax.nn.sigmoid(g)

        gla_outs = []
        for i, c in enumerate(chunks):
            rows = chunk_rows(c)
            outs = []
            for hd in range(GLA_HEADS):
                dk, dv = _head_dk(hd), _head_dv(hd)
                st = st_ref[hd]
                o_inter = _dot(qes[i][:, dk].astype(BF16), st.astype(BF16))
                ointer_ref[rows, dv] = o_inter
                sc = jnp.where(causal, scs[i][hd], 0.0).astype(BF16)
                outs.append(o_inter + _dot(sc, vb[local(c), dv]))
                e_col = jnp.sum(
                    jnp.where(eye, jnp.broadcast_to(e_lasts[i][:, dk], (GLA_DK, GLA_DK)), 0.0),
                    axis=1, keepdims=True)
                st_ref[hd] = st * e_col + _dot_tn(kds[i][:, dk].astype(BF16), vb[local(c), dv])
            gla_outs.append(outs)

        conv_parts = [project_rows(c, GLA_V_DIM, GLA_V_DIM + CONV_DIM) for c in chunks]
        for c, outs in zip(chunks, gla_outs):
            finish_heads(c, outs)
        project_out(chunks, conv_parts)
        return functools.reduce(jnp.maximum, [-bc[c_len - 1:c_len, :] for bc in bcs])

    per_sub = MIXER_SUB_TOKENS // c_len
    worst = functools.reduce(jnp.maximum, [
        sub_tile(c0 * c_len, list(range(c0, c0 + per_sub))) for c0 in range(0, n_chunks, per_sub)])
    xs_ref[0:SUBLANES, :] = xs_ref[ts:ts + SUBLANES, :]
    need_exact = jnp.max(worst) > SAFE_DECAY_EXPONENT

    @pl.when(need_exact)
    def _redo_intra_chunk_exactly():
        expand = jnp.where(
            lax.broadcasted_iota(jnp.int32, (GLA_QK_DIM, GLA_V_DIM), 0) // GLA_DK
            == lax.broadcasted_iota(jnp.int32, (GLA_QK_DIM, GLA_V_DIM), 1) // GLA_DV,
            1.0, 0.0).astype(BF16)
        row_id = lax.broadcasted_iota(jnp.int32, (c_len, 1), 0)
        for c in range(n_chunks):
            base = c * c_len
            rows = slice(base, base + c_len)
            acc_ref[...] = jnp.zeros_like(acc_ref)

            def add_source_row(s, carry):
                k_s = k_ref[pl.ds(base + s, 1), :]
                b_s = bc_ref[pl.ds(base + s, 1), :]
                v_s = v_ref[pl.ds(base + s, 1), :]
                w = q_ref[rows, :] * k_s * jnp.exp(jnp.minimum(bc_ref[rows, :] - b_s, 0.0))
                r = _dot(w.astype(BF16), expand)
                acc_ref[...] += jnp.where(row_id >= s, r, 0.0) * v_s
                return carry

            lax.fori_loop(0, c_len, add_source_row, 0)
            finish_heads(c, [acc_ref[:, _head_dv(hd)] + ointer_ref[rows, _head_dv(hd)]
                             for hd in range(GLA_HEADS)])
        project_out(range(n_chunks), [project_rows(c, GLA_V_DIM, GLA_V_DIM + CONV_DIM)
                                      for c in range(n_chunks)])


def _layer_block(shape, layer):
    return pl.BlockSpec((None,) + tuple(shape[1:]),
                        lambda *_: (layer,) + (0,) * (len(shape) - 1),
                        pipeline_mode=pl.Buffered(1))


def _mixer(h, layer, ln1, w_in, w_up, b_gate, conv_w, gn, w_out):
    b, s, d = h.shape
    ts = MIXER_TOKENS
    return pl.pallas_call(
        _mixer_body,
        grid=(b, s // ts),
        in_specs=[
            pl.BlockSpec((1, ts, d), lambda i, j: (i, j, 0)),
            _layer_block(ln1.shape, layer),
            _layer_block(w_in.shape, layer),
            _layer_block(w_up.shape, layer),
            _layer_block(b_gate.shape, layer),
            _layer_block(conv_w.shape, layer),
            _layer_block(gn.shape, layer),
            _layer_block(w_out.shape, layer),
        ],
        out_specs=pl.BlockSpec((1, ts, d), lambda i, j: (i, j, 0)),
        out_shape=jax.ShapeDtypeStruct(h.shape, h.dtype),
        scratch_shapes=[
            pltpu.VMEM((GLA_HEADS, GLA_DK, GLA_DV), F32),
            pltpu.VMEM((ts + SUBLANES, CONV_WIN), F32),
            pltpu.VMEM((ts, GLA_V_DIM + CONV_DIM), BF16),
            pltpu.VMEM((ts, GLA_QK_DIM), F32),
            pltpu.VMEM((ts, GLA_QK_DIM), F32),
            pltpu.VMEM((ts, GLA_QK_DIM), F32),
            pltpu.VMEM((ts, GLA_V_DIM), F32),
            pltpu.VMEM((ts, GLA_V_DIM), F32),
            pltpu.VMEM((ts, GLA_V_DIM), F32),
            pltpu.VMEM((GLA_CHUNK, GLA_V_DIM), F32),
        ],
        compiler_params=pltpu.CompilerParams(
            dimension_semantics=("arbitrary", "arbitrary"),
            vmem_limit_bytes=VMEM_LIMIT_BYTES),
        name="mixer",
    )(h, ln1, w_in, w_up, b_gate, conv_w, gn, w_out)


def _ffn_body(h_ref, p_ref, ln2_ref, wgu_ref, wdn_ref, ln3_ref, wpg_ref, wpp_ref, lnf_ref,
              o_ref, hid_ref, *, final):
    hidden = wdn_ref.shape[0]
    tm = h_ref.shape[0]

    halves = (slice(0, tm // 2), slice(tm // 2, tm))
    o_ref[...] = _dot(p_ref[...].astype(BF16), wpp_ref[...])
    u2 = _rmsnorm(h_ref[...], ln2_ref[...]).astype(BF16)
    for c in range(hidden // FFN_COLS):
        lo = c * FFN_COLS
        a = _dot(u2, wgu_ref[:, lo:lo + FFN_COLS])
        bb = _dot(u2, wgu_ref[:, hidden + lo:hidden + lo + FFN_COLS])
        hid_ref[:, lo:lo + FFN_COLS] = (a * jax.nn.sigmoid(a) * bb).astype(BF16)
    hs = [h_ref[r, :] + _dot(hid_ref[r, :], wdn_ref[...]) for r in halves]
    gates = [jax.nn.sigmoid(_dot(_rmsnorm(hr, ln3_ref[...]).astype(BF16), wpg_ref[...]))
             for hr in hs]
    for r, hr, gate in zip(halves, hs, gates):
        hr = hr + gate * o_ref[r, :]
        if final:
            hr = _rmsnorm(hr, lnf_ref[...])
        o_ref[r, :] = hr


def _ffn(h, p, layer, ln2, w_gu, w_dn, ln3, w_pg, w_pp, lnf, *, final):
    t, d = h.shape
    tm = FFN_TOKENS
    hidden = w_dn.shape[1]
    return pl.pallas_call(
        functools.partial(_ffn_body, final=final),
        grid=(t // tm,),
        in_specs=[
            pl.BlockSpec((tm, d), lambda i: (i, 0)),
            pl.BlockSpec((None, tm, p.shape[2]), lambda i: (layer, i, 0)),
            _layer_block(ln2.shape, layer),
            _layer_block(w_gu.shape, layer),
            _layer_block(w_dn.shape, layer),
            _layer_block(ln3.shape, layer),
            _layer_block(w_pg.shape, layer),
            _layer_block(w_pp.shape, layer),
            pl.BlockSpec(lnf.shape, lambda i: (0, 0)),
        ],
        out_specs=pl.BlockSpec((tm, d), lambda i: (i, 0)),
        out_shape=jax.ShapeDtypeStruct(h.shape, h.dtype),
        scratch_shapes=[pltpu.VMEM((tm, hidden), BF16)],
        compiler_params=pltpu.CompilerParams(
            dimension_semantics=("arbitrary",),
            vmem_limit_bytes=VMEM_LIMIT_BYTES),
        name="ffn",
    )(h, p, ln2, w_gu, w_dn, ln3, w_pg, w_pp, lnf)


def kernel(x, p, ln1_g, w_in, w_gate_up, b_gate, conv_w, gn_g, w_out, ln2_g,
           w_gate_upffn, w_down, ln3_g, w_ple_gate, w_ple_proj, lnf_g):
    b, s, d = x.shape
    depth = w_in.shape[0]
    assert s % MIXER_TOKENS == 0 and MIXER_TOKENS % MIXER_SUB_TOKENS == 0
    assert MIXER_SUB_TOKENS % GLA_CHUNK == 0
    assert (b * s) % FFN_TOKENS == 0 and w_down.shape[1] % FFN_COLS == 0
    assert w_in.shape[2] == TAIL_OFF + GATE_RANK + CONV_WIDTH * CONV_DIM

    w_in_b = jnp.pad(w_in.astype(BF16), ((0, 0), (0, 0), (0, IN_DIM_PADDED - w_in.shape[2])))
    w_up_b = jnp.pad(w_gate_up.astype(BF16), ((0, 0), (0, LANES - GATE_RANK), (0, 0)))
    conv_w_p = jnp.pad(conv_w, ((0, 0), (0, 0), (GATE_RANK, CONV_WIN - CONV_DIM - GATE_RANK)))
    w_out_b = w_out.astype(BF16)
    w_gu_b = w_gate_upffn.astype(BF16)
    w_dn_b = w_down.astype(BF16)
    w_pg_b = w_ple_gate.astype(BF16)
    w_pp_b = w_ple_proj.astype(BF16)
    rowvec = lambda a: a[:, None, :]
    p_flat = p.reshape(depth, b * s, p.shape[-1])

    h = x
    for i in range(depth):
        h = _mixer(h, i, rowvec(ln1_g), w_in_b, w_up_b, rowvec(b_gate), conv_w_p,
                   rowvec(gn_g), w_out_b)
        h = _ffn(h.reshape(b * s, d), p_flat, i, rowvec(ln2_g), w_gu_b, w_dn_b, rowvec(ln3_g),
                 w_pg_b, w_pp_b, lnf_g[None, :], final=(i == depth - 1)).reshape(b, s, d)
    return h
```

```python
import functools

import jax
import jax.numpy as jnp
from jax import lax
from jax.experimental import pallas as pl
from jax.experimental.pallas import tpu as pltpu

F32 = jnp.float32
BF16 = jnp.bfloat16

GLA_HEADS = 4
GLA_DK = 64
GLA_DV = 128
GLA_QK_DIM = GLA_HEADS * GLA_DK
GLA_V_DIM = GLA_HEADS * GLA_DV
GATE_RANK = 16
GATE_NORMALIZER = 16.0
CONV_DIM = 512
CONV_WIDTH = 3
NORM_EPS = 1e-6

LANES = 128
SUBLANES = 8
BF16_ROWS = 16
VMEM_LIMIT_BYTES = 56 * 1024 * 1024

Q_OFF = 0
K_OFF = Q_OFF + GLA_QK_DIM
V_OFF = K_OFF + GLA_QK_DIM
G_OFF = V_OFF + GLA_V_DIM
TAIL_OFF = G_OFF + GLA_V_DIM
TAIL_DIM = -(-(GATE_RANK + 3 * CONV_DIM) // LANES) * LANES
IN_DIM_PADDED = TAIL_OFF + TAIL_DIM
CONV_WIN = CONV_DIM + LANES

MIXER_TOKENS = 1024
MIXER_SUB_TOKENS = 512
GLA_CHUNK = 256
FFN_TOKENS = 1024
FFN_COLS = 256
CAST_STEPS = 32

SAFE_DECAY_EXPONENT = 40.0


def _rmsnorm(x, g):
    ms = jnp.mean(x * x, axis=-1, keepdims=True)
    return x * lax.rsqrt(ms + NORM_EPS) * g


_dot = functools.partial(jnp.dot, preferred_element_type=F32)
_dot_nt = functools.partial(lax.dot_general, dimension_numbers=(((1,), (1,)), ((), ())),
                            preferred_element_type=F32)
_dot_tn = functools.partial(lax.dot_general, dimension_numbers=(((0,), (0,)), ((), ())),
                            preferred_element_type=F32)


def _head_dk(hd):
    return slice(hd * GLA_DK, (hd + 1) * GLA_DK)


def _head_dv(hd):
    return slice(hd * GLA_DV, (hd + 1) * GLA_DV)


def _mixer_body(h_ref, ln1_ref, win_ref, wup_ref, bg_ref, cw_ref, gn_ref, wout_ref,
                o_ref, st_ref, xs_ref, mix_ref,
                q_ref, k_ref, bc_ref, v_ref, gate_ref, ointer_ref, acc_ref):
    ts = h_ref.shape[1]
    c_len = GLA_CHUNK
    n_chunks = ts // c_len

    @pl.when(pl.program_id(1) == 0)
    def _reset_sequence_state():
        st_ref[...] = jnp.zeros_like(st_ref)
        xs_ref[0:SUBLANES, :] = jnp.zeros((SUBLANES, CONV_WIN), F32)

    def chunk_rows(c):
        return slice(c * c_len, (c + 1) * c_len)

    row = lax.broadcasted_iota(jnp.int32, (c_len, c_len), 0)
    col = lax.broadcasted_iota(jnp.int32, (c_len, c_len), 1)
    causal = row >= col
    tri = jnp.where(causal, 1.0, 0.0).astype(BF16)
    eye = (lax.broadcasted_iota(jnp.int32, (GLA_DK, GLA_DK), 0)
           == lax.broadcasted_iota(jnp.int32, (GLA_DK, GLA_DK), 1))

    def finish_heads(c, outs):
        rows = chunk_rows(c)
        for hd, o in enumerate(outs):
            dv = _head_dv(hd)
            on = _rmsnorm(o, gn_ref[...]) * gate_ref[rows, dv]
            mix_ref[rows, dv] = on.astype(BF16)

    def project_rows(c, lo, hi):
        return _dot(mix_ref[chunk_rows(c), lo:hi], wout_ref[lo:hi, :])

    def project_out(chunks, conv_parts):
        for c, conv_part in zip(chunks, conv_parts):
            rows = chunk_rows(c)
            o_ref[0, rows, :] = (h_ref[0, rows, :] + conv_part
                                 + project_rows(c, 0, GLA_V_DIM))

    def sub_tile(r0, chunks):
        sub = len(chunks) * c_len
        srows = slice(r0, r0 + sub)
        u = _rmsnorm(h_ref[0, srows, :], ln1_ref[...]).astype(BF16)

        def proj(off, width):
            return _dot(u, win_ref[:, off:off + width])

        def local(c):
            return slice(c * c_len - r0, (c + 1) * c_len - r0)

        t0 = proj(TAIL_OFF, LANES)
        q = proj(Q_OFF, GLA_QK_DIM) * (GLA_DK ** -0.5)
        logits = _dot(t0.astype(BF16), wup_ref[...]) + bg_ref[...]
        k = proj(K_OFF, GLA_QK_DIM)
        t1a = proj(TAIL_OFF + LANES, CONV_DIM)
        t1b = proj(TAIL_OFF + LANES + CONV_DIM, CONV_DIM)
        t1c = proj(TAIL_OFF + LANES + 2 * CONV_DIM, CONV_DIM)
        log_a = ((jnp.minimum(logits, 0.0) - jnp.log1p(jnp.exp(-jnp.abs(logits))))
                 * (1.0 / GATE_NORMALIZER))

        cb = jnp.concatenate([t0, t1a], axis=1)
        xin = (jnp.concatenate([t1a[:, CONV_DIM - LANES:], t1b], axis=1)
               * jnp.concatenate([t1b[:, CONV_DIM - LANES:], t1c], axis=1))
        xs_ref[SUBLANES + r0:SUBLANES + r0 + sub, :] = xin
        xe = xs_ref[r0:r0 + sub + SUBLANES, :]
        x1 = pltpu.roll(xe, 1, 0)[SUBLANES:, :]
        x2 = pltpu.roll(xe, 2, 0)[SUBLANES:, :]
        y = cb * (cw_ref[0:1, :] * x2 + cw_ref[1:2, :] * x1 + cw_ref[2:3, :] * xin)
        y = pltpu.roll(y, CONV_WIN - GATE_RANK, 1)[:, 0:CONV_DIM]
        mix_ref[srows, GLA_V_DIM:GLA_V_DIM + CONV_DIM] = y.astype(BF16)

        la_hi = log_a.astype(BF16)
        la_lo = (log_a - la_hi.astype(F32)).astype(BF16)
        bcs = [_dot(tri, la_hi[local(c)]) + _dot(tri, la_lo[local(c)]) for c in chunks]
        v = proj(V_OFF, GLA_V_DIM)
        g = proj(G_OFF, GLA_V_DIM)
        gate_ref[srows, :] = g * jax.nn.sigmoid(g)

        q_ref[srows, :] = q
        k_ref[srows, :] = k
        v_ref[srows, :] = v
        vb = v.astype(BF16)
        qes, kes, kds, e_lasts = [], [], [], []
        for c, bc in zip(chunks, bcs):
            bc_ref[chunk_rows(c), :] = bc
            b_last = bc[c_len - 1:c_len, :]
            qes.append(q[local(c)] * jnp.exp(bc))
            kes.append(k[local(c)] * jnp.exp(-bc))
            kds.append(k[local(c)] * jnp.exp(b_last - bc))
            e_lasts.append(jnp.exp(b_last))

        scs = [[_dot_nt(qe[:, _head_dk(hd)].astype(BF16), ke[:, _head_dk(hd)].astype(BF16))
                for hd in range(GLA_HEADS)] for qe, ke in zip(qes, kes)]

        gla_outs = []
        for i, c in enumerate(chunks):
            rows = chunk_rows(c)
            outs = []
            for hd in range(GLA_HEADS):
                dk, dv = _head_dk(hd), _head_dv(hd)
                st = st_ref[hd]
                o_inter = _dot(qes[i][:, dk].astype(BF16), st.astype(BF16))
                ointer_ref[rows, dv] = o_inter
                sc = jnp.where(causal, scs[i][hd], 0.0).astype(BF16)
                outs.append(o_inter + _dot(sc, vb[local(c), dv]))
                e_col = jnp.sum(
                    jnp.where(eye, jnp.broadcast_to(e_lasts[i][:, dk], (GLA_DK, GLA_DK)), 0.0),
                    axis=1, keepdims=True)
                st_ref[hd] = st * e_col + _dot_tn(kds[i][:, dk].astype(BF16), vb[local(c), dv])
            gla_outs.append(outs)

        conv_parts = [project_rows(c, GLA_V_DIM, GLA_V_DIM + CONV_DIM) for c in chunks]
        for c, outs in zip(chunks, gla_outs):
            finish_heads(c, outs)
        project_out(chunks, conv_parts)
        return functools.reduce(jnp.maximum, [-bc[c_len - 1:c_len, :] for bc in bcs])

    per_sub = MIXER_SUB_TOKENS // c_len
    worst = functools.reduce(jnp.maximum, [
        sub_tile(c0 * c_len, list(range(c0, c0 + per_sub))) for c0 in range(0, n_chunks, per_sub)])
    xs_ref[0:SUBLANES, :] = xs_ref[ts:ts + SUBLANES, :]
    need_exact = jnp.max(worst) > SAFE_DECAY_EXPONENT

    @pl.when(need_exact)
    def _redo_intra_chunk_exactly():
        expand = jnp.where(
            lax.broadcasted_iota(jnp.int32, (GLA_QK_DIM, GLA_V_DIM), 0) // GLA_DK
            == lax.broadcasted_iota(jnp.int32, (GLA_QK_DIM, GLA_V_DIM), 1) // GLA_DV,
            1.0, 0.0).astype(BF16)
        row_id = lax.broadcasted_iota(jnp.int32, (c_len, 1), 0)
        for c in range(n_chunks):
            base = c * c_len
            rows = slice(base, base + c_len)
            acc_ref[...] = jnp.zeros_like(acc_ref)

            def add_source_row(s, carry):
                k_s = k_ref[pl.ds(base + s, 1), :]
                b_s = bc_ref[pl.ds(base + s, 1), :]
                v_s = v_ref[pl.ds(base + s, 1), :]
                w = q_ref[rows, :] * k_s * jnp.exp(jnp.minimum(bc_ref[rows, :] - b_s, 0.0))
                r = _dot(w.astype(BF16), expand)
                acc_ref[...] += jnp.where(row_id >= s, r, 0.0) * v_s
                return carry

            lax.fori_loop(0, c_len, add_source_row, 0)
            finish_heads(c, [acc_ref[:, _head_dv(hd)] + ointer_ref[rows, _head_dv(hd)]
                             for hd in range(GLA_HEADS)])
        project_out(range(n_chunks), [project_rows(c, GLA_V_DIM, GLA_V_DIM + CONV_DIM)
                                      for c in range(n_chunks)])


def _layer_block(shape, layer):
    return pl.BlockSpec((None,) + tuple(shape[1:]),
                        lambda *_: (layer,) + (0,) * (len(shape) - 1),
                        pipeline_mode=pl.Buffered(1))


def _resident(shape):
    return pl.BlockSpec(tuple(shape), lambda *_: (0,) * len(shape), pipeline_mode=pl.Buffered(1))


def _cast_rows(n_rows, n_steps):
    stride = 1
    while stride <= n_steps:
        n_blocks = n_steps // stride
        if n_steps % stride == 0 and n_rows % n_blocks == 0 and (n_rows // n_blocks) % BF16_ROWS == 0:
            return n_rows // n_blocks, stride
        stride *= 2
    raise ValueError(f"no aligned row split of {n_rows} over {n_steps} steps")


def _cast_specs(srcs, layer, n_steps, step_of, out_widths):
    in_specs, out_specs, out_shapes = [], [], []
    for w, width in zip(srcs, out_widths):
        rows, stride = _cast_rows(w.shape[1], n_steps)
        in_specs.append(pl.BlockSpec(
            (None, rows, w.shape[2]), lambda *g, st=stride: (layer, step_of(*g) // st, 0)))
        out_specs.append(pl.BlockSpec((rows, width), lambda *g, st=stride: (step_of(*g) // st, 0)))
        out_shapes.append(jax.ShapeDtypeStruct((w.shape[1], width), BF16))
    return in_specs, out_specs, out_shapes


def _cast_block(src_ref, dst_ref):
    w_src, w_dst = src_ref.shape[-1], dst_ref.shape[-1]
    if w_src == w_dst:
        dst_ref[...] = src_ref[...].astype(BF16)
        return
    full = w_src // LANES * LANES
    dst_ref[:, 0:full] = src_ref[:, 0:full].astype(BF16)
    tail = jnp.pad(src_ref[:, full:w_src], ((0, 0), (0, w_dst - w_src)))
    dst_ref[:, full:] = tail.astype(BF16)


def _with_casts(body, n_in, n_cast):
    def kernel_fn(*refs):
        ins, cast_in = refs[:n_in], refs[n_in:n_in + n_cast]
        out = refs[n_in + n_cast]
        cast_out = refs[n_in + n_cast + 1:n_in + 2 * n_cast + 1]
        scratch = refs[n_in + 2 * n_cast + 1:]
        for src, dst in zip(cast_in, cast_out):
            _cast_block(src, dst)
        body(*ins, out, *scratch)
    return kernel_fn


def _lane_padded(w):
    return -(-w.shape[2] // LANES) * LANES


def _cast_only(srcs, layer):
    def body(*refs):
        for src, dst in zip(refs[:len(srcs)], refs[len(srcs):]):
            _cast_block(src, dst)
    cast_in, cast_out, cast_shapes = _cast_specs(
        srcs, layer, CAST_STEPS, lambda i: i, [_lane_padded(w) for w in srcs])
    return pl.pallas_call(
        body, grid=(CAST_STEPS,), in_specs=cast_in, out_specs=cast_out, out_shape=cast_shapes,
        compiler_params=pltpu.CompilerParams(dimension_semantics=("arbitrary",)),
        name="cast_weights",
    )(*srcs)


def _mixer(h, layer, ln1, w_in, w_up, b_gate, conv_w, gn, w_out, ffn_weights):
    b, s, d = h.shape
    ts = MIXER_TOKENS
    n_seq = s // ts
    cast_in, cast_out, cast_shapes = _cast_specs(
        ffn_weights, layer, b * n_seq, lambda i, j: i * n_seq + j,
        [w.shape[2] for w in ffn_weights])
    ins = [h, ln1, w_in, w_up, b_gate, conv_w, gn, w_out]
    return pl.pallas_call(
        _with_casts(_mixer_body, len(ins), len(ffn_weights)),
        grid=(b, n_seq),
        in_specs=[
            pl.BlockSpec((1, ts, d), lambda i, j: (i, j, 0)),
            _layer_block(ln1.shape, layer),
            _resident(w_in.shape),
            _layer_block(w_up.shape, layer),
            _layer_block(b_gate.shape, layer),
            _layer_block(conv_w.shape, layer),
            _layer_block(gn.shape, layer),
            _resident(w_out.shape),
        ] + cast_in,
        out_specs=[pl.BlockSpec((1, ts, d), lambda i, j: (i, j, 0))] + cast_out,
        out_shape=[jax.ShapeDtypeStruct(h.shape, h.dtype)] + cast_shapes,
        scratch_shapes=[
            pltpu.VMEM((GLA_HEADS, GLA_DK, GLA_DV), F32),
            pltpu.VMEM((ts + SUBLANES, CONV_WIN), F32),
            pltpu.VMEM((ts, GLA_V_DIM + CONV_DIM), BF16),
            pltpu.VMEM((ts, GLA_QK_DIM), F32),
            pltpu.VMEM((ts, GLA_QK_DIM), F32),
            pltpu.VMEM((ts, GLA_QK_DIM), F32),
            pltpu.VMEM((ts, GLA_V_DIM), F32),
            pltpu.VMEM((ts, GLA_V_DIM), F32),
            pltpu.VMEM((ts, GLA_V_DIM), F32),
            pltpu.VMEM((GLA_CHUNK, GLA_V_DIM), F32),
        ],
        compiler_params=pltpu.CompilerParams(
            dimension_semantics=("arbitrary", "arbitrary"),
            vmem_limit_bytes=VMEM_LIMIT_BYTES),
        name="mixer",
    )(*ins, *ffn_weights)


def _ffn_body(h_ref, p_ref, ln2_ref, wgu_ref, wdn_ref, ln3_ref, wpg_ref, wpp_ref, lnf_ref,
              o_ref, hid_ref, *, final):
    hidden = wdn_ref.shape[0]
    tm = h_ref.shape[0]

    halves = (slice(0, tm // 2), slice(tm // 2, tm))
    o_ref[...] = _dot(p_ref[...].astype(BF16), wpp_ref[...])
    u2 = _rmsnorm(h_ref[...], ln2_ref[...]).astype(BF16)
    for c in range(hidden // FFN_COLS):
        lo = c * FFN_COLS
        a = _dot(u2, wgu_ref[:, lo:lo + FFN_COLS])
        bb = _dot(u2, wgu_ref[:, hidden + lo:hidden + lo + FFN_COLS])
        hid_ref[:, lo:lo + FFN_COLS] = (a * jax.nn.sigmoid(a) * bb).astype(BF16)
    hs = [h_ref[r, :] + _dot(hid_ref[r, :], wdn_ref[...]) for r in halves]
    gates = [jax.nn.sigmoid(_dot(_rmsnorm(hr, ln3_ref[...]).astype(BF16), wpg_ref[...]))
             for hr in hs]
    for r, hr, gate in zip(halves, hs, gates):
        hr = hr + gate * o_ref[r, :]
        if final:
            hr = _rmsnorm(hr, lnf_ref[...])
        o_ref[r, :] = hr


def _ffn(h, p, layer, ln2, w_gu, w_dn, ln3, w_pg, w_pp, lnf, next_mixer_weights, *, final):
    t, d = h.shape
    tm = FFN_TOKENS
    hidden = w_dn.shape[0]
    cast_in, cast_out, cast_shapes = _cast_specs(
        next_mixer_weights, layer + 1, t // tm, lambda i: i,
        [_lane_padded(w) for w in next_mixer_weights])
    ins = [h, p, ln2, w_gu, w_dn, ln3, w_pg, w_pp, lnf]
    return pl.pallas_call(
        _with_casts(functools.partial(_ffn_body, final=final), len(ins), len(next_mixer_weights)),
        grid=(t // tm,),
        in_specs=[
            pl.BlockSpec((tm, d), lambda i: (i, 0)),
            pl.BlockSpec((None, tm, p.shape[2]), lambda i: (layer, i, 0)),
            _layer_block(ln2.shape, layer),
            _resident(w_gu.shape),
            _resident(w_dn.shape),
            _layer_block(ln3.shape, layer),
            _resident(w_pg.shape),
            _resident(w_pp.shape),
            pl.BlockSpec(lnf.shape, lambda i: (0, 0)),
        ] + cast_in,
        out_specs=[pl.BlockSpec((tm, d), lambda i: (i, 0))] + cast_out,
        out_shape=[jax.ShapeDtypeStruct(h.shape, h.dtype)] + cast_shapes,
        scratch_shapes=[pltpu.VMEM((tm, hidden), BF16)],
        compiler_params=pltpu.CompilerParams(
            dimension_semantics=("arbitrary",),
            vmem_limit_bytes=VMEM_LIMIT_BYTES),
        name="ffn",
    )(*ins, *next_mixer_weights)


def kernel(x, p, ln1_g, w_in, w_gate_up, b_gate, conv_w, gn_g, w_out, ln2_g,
           w_gate_upffn, w_down, ln3_g, w_ple_gate, w_ple_proj, lnf_g):
    b, s, d = x.shape
    depth = w_in.shape[0]
    assert s % MIXER_TOKENS == 0 and MIXER_TOKENS % MIXER_SUB_TOKENS == 0
    assert MIXER_SUB_TOKENS % GLA_CHUNK == 0
    assert (b * s) % FFN_TOKENS == 0 and w_down.shape[1] % FFN_COLS == 0
    assert w_in.shape[2] == TAIL_OFF + GATE_RANK + CONV_WIDTH * CONV_DIM

    assert _lane_padded(w_in) == IN_DIM_PADDED
    w_in_b, w_out_b = _cast_only((w_in, w_out), 0)
    w_up_b = jnp.pad(w_gate_up.astype(BF16), ((0, 0), (0, LANES - GATE_RANK), (0, 0)))
    conv_w_p = jnp.pad(conv_w, ((0, 0), (0, 0), (GATE_RANK, CONV_WIN - CONV_DIM - GATE_RANK)))
    rowvec = lambda a: a[:, None, :]
    p_flat = p.reshape(depth, b * s, p.shape[-1])

    h = x
    for i in range(depth):
        last = i == depth - 1
        h, w_gu_b, w_dn_b, w_pg_b, w_pp_b = _mixer(
            h, i, rowvec(ln1_g), w_in_b, w_up_b, rowvec(b_gate), conv_w_p, rowvec(gn_g), w_out_b,
            (w_gate_upffn, w_down, w_ple_gate, w_ple_proj))
        h, *next_mixer = _ffn(
            h.reshape(b * s, d), p_flat, i, rowvec(ln2_g), w_gu_b, w_dn_b, rowvec(ln3_g),
            w_pg_b, w_pp_b, lnf_g[None, :], () if last else (w_in, w_out), final=last)
        h = h.reshape(b, s, d)
        if not last:
            w_in_b, w_out_b = next_mixer
    return h
```

```python
import functools

import jax
import jax.numpy as jnp
from jax import lax
from jax.experimental import pallas as pl
from jax.experimental.pallas import tpu as pltpu

F32 = jnp.float32
BF16 = jnp.bfloat16

GLA_HEADS = 4
GLA_DK = 64
GLA_DV = 128
GLA_QK_DIM = GLA_HEADS * GLA_DK
GLA_V_DIM = GLA_HEADS * GLA_DV
GATE_RANK = 16
GATE_NORMALIZER = 16.0
CONV_DIM = 512
CONV_WIDTH = 3
NORM_EPS = 1e-6

LANES = 128
SUBLANES = 8
BF16_ROWS = 16
VMEM_LIMIT_BYTES = 56 * 1024 * 1024

Q_OFF = 0
K_OFF = Q_OFF + GLA_QK_DIM
V_OFF = K_OFF + GLA_QK_DIM
G_OFF = V_OFF + GLA_V_DIM
TAIL_OFF = G_OFF + GLA_V_DIM
TAIL_DIM = -(-(GATE_RANK + 3 * CONV_DIM) // LANES) * LANES
IN_DIM_PADDED = TAIL_OFF + TAIL_DIM
CONV_WIN = CONV_DIM + LANES

MIXER_TOKENS = 1024
MIXER_SUB_TOKENS = 512
GLA_CHUNK = 256
FFN_TOKENS = 1024
FFN_COLS = 256
CAST_STEPS = 8

SAFE_DECAY_EXPONENT = 40.0


def _rmsnorm(x, g):
    ms = jnp.mean(x * x, axis=-1, keepdims=True)
    return x * lax.rsqrt(ms + NORM_EPS) * g


_dot = functools.partial(jnp.dot, preferred_element_type=F32)
_dot_nt = functools.partial(lax.dot_general, dimension_numbers=(((1,), (1,)), ((), ())),
                            preferred_element_type=F32)
_dot_tn = functools.partial(lax.dot_general, dimension_numbers=(((0,), (0,)), ((), ())),
                            preferred_element_type=F32)


def _head_dk(hd):
    return slice(hd * GLA_DK, (hd + 1) * GLA_DK)


def _head_dv(hd):
    return slice(hd * GLA_DV, (hd + 1) * GLA_DV)


def _mixer_body(h_ref, ln1_ref, win_ref, wup_ref, bg_ref, cw_ref, gn_ref, wout_ref,
                o_ref, st_ref, xs_ref, mix_ref,
                q_ref, k_ref, bc_ref, v_ref, gate_ref, ointer_ref, acc_ref):
    ts = h_ref.shape[1]
    c_len = GLA_CHUNK
    n_chunks = ts // c_len

    @pl.when(pl.program_id(1) == 0)
    def _reset_sequence_state():
        st_ref[...] = jnp.zeros_like(st_ref)
        xs_ref[0:SUBLANES, :] = jnp.zeros((SUBLANES, CONV_WIN), F32)

    def chunk_rows(c):
        return slice(c * c_len, (c + 1) * c_len)

    row = lax.broadcasted_iota(jnp.int32, (c_len, c_len), 0)
    col = lax.broadcasted_iota(jnp.int32, (c_len, c_len), 1)
    causal = row >= col
    tri = jnp.where(causal, 1.0, 0.0).astype(BF16)
    eye = (lax.broadcasted_iota(jnp.int32, (GLA_DK, GLA_DK), 0)
           == lax.broadcasted_iota(jnp.int32, (GLA_DK, GLA_DK), 1))

    def finish_heads(c, outs):
        rows = chunk_rows(c)
        for hd, o in enumerate(outs):
            dv = _head_dv(hd)
            on = _rmsnorm(o, gn_ref[...]) * gate_ref[rows, dv]
            mix_ref[rows, dv] = on.astype(BF16)

    def project_rows(c, lo, hi):
        return _dot(mix_ref[chunk_rows(c), lo:hi], wout_ref[lo:hi, :])

    def project_out(chunks, conv_parts):
        for c, conv_part in zip(chunks, conv_parts):
            rows = chunk_rows(c)
            o_ref[0, rows, :] = (h_ref[0, rows, :] + conv_part
                                 + project_rows(c, 0, GLA_V_DIM))

    def sub_tile(r0, chunks):
        sub = len(chunks) * c_len
        srows = slice(r0, r0 + sub)
        u = _rmsnorm(h_ref[0, srows, :], ln1_ref[...]).astype(BF16)

        def proj(off, width):
            return _dot(u, win_ref[:, off:off + width])

        def local(c):
            return slice(c * c_len - r0, (c + 1) * c_len - r0)

        t0 = proj(TAIL_OFF, LANES)
        q = proj(Q_OFF, GLA_QK_DIM) * (GLA_DK ** -0.5)
        logits = _dot(t0.astype(BF16), wup_ref[...]) + bg_ref[...]
        k = proj(K_OFF, GLA_QK_DIM)
        t1a = proj(TAIL_OFF + LANES, CONV_DIM)
        t1b = proj(TAIL_OFF + LANES + CONV_DIM, CONV_DIM)
        t1c = proj(TAIL_OFF + LANES + 2 * CONV_DIM, CONV_DIM)
        log_a = ((jnp.minimum(logits, 0.0) - jnp.log1p(jnp.exp(-jnp.abs(logits))))
                 * (1.0 / GATE_NORMALIZER))

        cb = jnp.concatenate([t0, t1a], axis=1)
        xin = (jnp.concatenate([t1a[:, CONV_DIM - LANES:], t1b], axis=1)
               * jnp.concatenate([t1b[:, CONV_DIM - LANES:], t1c], axis=1))
        xs_ref[SUBLANES + r0:SUBLANES + r0 + sub, :] = xin
        xe = xs_ref[r0:r0 + sub + SUBLANES, :]
        x1 = pltpu.roll(xe, 1, 0)[SUBLANES:, :]
        x2 = pltpu.roll(xe, 2, 0)[SUBLANES:, :]
        y = cb * (cw_ref[0:1, :] * x2 + cw_ref[1:2, :] * x1 + cw_ref[2:3, :] * xin)
        y = pltpu.roll(y, CONV_WIN - GATE_RANK, 1)[:, 0:CONV_DIM]
        mix_ref[srows, GLA_V_DIM:GLA_V_DIM + CONV_DIM] = y.astype(BF16)

        la_hi = log_a.astype(BF16)
        la_lo = (log_a - la_hi.astype(F32)).astype(BF16)
        bcs = [_dot(tri, la_hi[local(c)]) + _dot(tri, la_lo[local(c)]) for c in chunks]
        v = proj(V_OFF, GLA_V_DIM)
        g = proj(G_OFF, GLA_V_DIM)
        gate_ref[srows, :] = g * jax.nn.sigmoid(g)

        q_ref[srows, :] = q
        k_ref[srows, :] = k
        v_ref[srows, :] = v
        vb = v.astype(BF16)
        qes, kes, kds, e_lasts = [], [], [], []
        for c, bc in zip(chunks, bcs):
            bc_ref[chunk_rows(c), :] = bc
            b_last = bc[c_len - 1:c_len, :]
            qes.append(q[local(c)] * jnp.exp(bc))
            kes.append(k[local(c)] * jnp.exp(-bc))
            kds.append(k[local(c)] * jnp.exp(b_last - bc))
            e_lasts.append(jnp.exp(b_last))

        scs = [[_dot_nt(qe[:, _head_dk(hd)].astype(BF16), ke[:, _head_dk(hd)].astype(BF16))
                for hd in range(GLA_HEADS)] for qe, ke in zip(qes, kes)]

        gla_outs = []
        for i, c in enumerate(chunks):
            rows = chunk_rows(c)
            outs = []
            for hd in range(GLA_HEADS):
                dk, dv = _head_dk(hd), _head_dv(hd)
                st = st_ref[hd]
                o_inter = _dot(qes[i][:, dk].astype(BF16), st.astype(BF16))
                ointer_ref[rows, dv] = o_inter
                sc = jnp.where(causal, scs[i][hd], 0.0).astype(BF16)
                outs.append(o_inter + _dot(sc, vb[local(c), dv]))
                e_col = jnp.sum(
                    jnp.where(eye, jnp.broadcast_to(e_lasts[i][:, dk], (GLA_DK, GLA_DK)), 0.0),
                    axis=1, keepdims=True)
                st_ref[hd] = st * e_col + _dot_tn(kds[i][:, dk].astype(BF16), vb[local(c), dv])
            gla_outs.append(outs)

        conv_parts = [project_rows(c, GLA_V_DIM, GLA_V_DIM + CONV_DIM) for c in chunks]
        for c, outs in zip(chunks, gla_outs):
            finish_heads(c, outs)
        project_out(chunks, conv_parts)
        return functools.reduce(jnp.maximum, [-bc[c_len - 1:c_len, :] for bc in bcs])

    per_sub = MIXER_SUB_TOKENS // c_len
    worst = functools.reduce(jnp.maximum, [
        sub_tile(c0 * c_len, list(range(c0, c0 + per_sub))) for c0 in range(0, n_chunks, per_sub)])
    xs_ref[0:SUBLANES, :] = xs_ref[ts:ts + SUBLANES, :]
    need_exact = jnp.max(worst) > SAFE_DECAY_EXPONENT

    @pl.when(need_exact)
    def _redo_intra_chunk_exactly():
        expand = jnp.where(
            lax.broadcasted_iota(jnp.int32, (GLA_QK_DIM, GLA_V_DIM), 0) // GLA_DK
            == lax.broadcasted_iota(jnp.int32, (GLA_QK_DIM, GLA_V_DIM), 1) // GLA_DV,
            1.0, 0.0).astype(BF16)
        row_id = lax.broadcasted_iota(jnp.int32, (c_len, 1), 0)
        for c in range(n_chunks):
            base = c * c_len
            rows = slice(base, base + c_len)
            acc_ref[...] = jnp.zeros_like(acc_ref)

            def add_source_row(s, carry):
                k_s = k_ref[pl.ds(base + s, 1), :]
                b_s = bc_ref[pl.ds(base + s, 1), :]
                v_s = v_ref[pl.ds(base + s, 1), :]
                w = q_ref[rows, :] * k_s * jnp.exp(jnp.minimum(bc_ref[rows, :] - b_s, 0.0))
                r = _dot(w.astype(BF16), expand)
                acc_ref[...] += jnp.where(row_id >= s, r, 0.0) * v_s
                return carry

            lax.fori_loop(0, c_len, add_source_row, 0)
            finish_heads(c, [acc_ref[:, _head_dv(hd)] + ointer_ref[rows, _head_dv(hd)]
                             for hd in range(GLA_HEADS)])
        project_out(range(n_chunks), [project_rows(c, GLA_V_DIM, GLA_V_DIM + CONV_DIM)
                                      for c in range(n_chunks)])


def _layer_block(shape, layer):
    return pl.BlockSpec((None,) + tuple(shape[1:]),
                        lambda *_: (layer,) + (0,) * (len(shape) - 1),
                        pipeline_mode=pl.Buffered(1))


def _resident(shape):
    return pl.BlockSpec(tuple(shape), lambda *_: (0,) * len(shape), pipeline_mode=pl.Buffered(1))


def _cast_rows(n_rows, n_steps):
    stride = 1
    while stride <= n_steps:
        n_blocks = n_steps // stride
        if n_steps % stride == 0 and n_rows % n_blocks == 0 and (n_rows // n_blocks) % BF16_ROWS == 0:
            return n_rows // n_blocks, stride
        stride *= 2
    raise ValueError(f"no aligned row split of {n_rows} over {n_steps} steps")


def _cast_specs(srcs, layer, n_steps, step_of):
    in_specs, out_specs, out_shapes = [], [], []
    for w, swapped in srcs:
        n_rows, n_cols = (w.shape[2], w.shape[1]) if swapped else (w.shape[1], w.shape[2])
        width = -(-n_cols // LANES) * LANES
        if swapped:
            rows, stride = LANES, n_steps // (n_rows // LANES)
            assert n_rows % LANES == 0 and n_steps % (n_rows // LANES) == 0
            in_specs.append(pl.BlockSpec(
                (None, n_cols, rows), lambda *g, st=stride: (layer, 0, step_of(*g) // st)))
        else:
            rows, stride = _cast_rows(n_rows, n_steps)
            in_specs.append(pl.BlockSpec(
                (None, rows, n_cols), lambda *g, st=stride: (layer, step_of(*g) // st, 0)))
        out_specs.append(pl.BlockSpec((rows, width), lambda *g, st=stride: (step_of(*g) // st, 0)))
        out_shapes.append(jax.ShapeDtypeStruct((n_rows, width), BF16))
    return in_specs, out_specs, out_shapes


def _cast_block(src_ref, dst_ref):
    if src_ref.shape == dst_ref.shape:
        dst_ref[...] = src_ref[...].astype(BF16)
        return
    n_cols = src_ref.shape[0]
    assert src_ref.shape[1] == dst_ref.shape[0] == LANES
    full = n_cols // LANES * LANES
    for t in range(0, full, LANES):
        dst_ref[:, t:t + LANES] = src_ref[t:t + LANES, :].T.astype(BF16)
    if full < n_cols:
        tail = jnp.pad(src_ref[full:n_cols, :], ((0, full + LANES - n_cols), (0, 0)))
        dst_ref[:, full:] = tail.T.astype(BF16)


def _with_casts(body, n_in, n_cast):
    def kernel_fn(*refs):
        ins, cast_in = refs[:n_in], refs[n_in:n_in + n_cast]
        out = refs[n_in + n_cast]
        cast_out = refs[n_in + n_cast + 1:n_in + 2 * n_cast + 1]
        scratch = refs[n_in + 2 * n_cast + 1:]
        for src, dst in zip(cast_in, cast_out):
            _cast_block(src, dst)
        body(*ins, out, *scratch)
    return kernel_fn


def _cast_only(srcs, layer):
    def body(*refs):
        for src, dst in zip(refs[:len(srcs)], refs[len(srcs):]):
            _cast_block(src, dst)
    cast_in, cast_out, cast_shapes = _cast_specs(srcs, layer, CAST_STEPS, lambda i: i)
    return pl.pallas_call(
        body, grid=(CAST_STEPS,), in_specs=cast_in, out_specs=cast_out, out_shape=cast_shapes,
        compiler_params=pltpu.CompilerParams(dimension_semantics=("arbitrary",)),
        name="cast_weights",
    )(*[w for w, _ in srcs])


def _mixer(h, layer, ln1, w_in, w_up, b_gate, conv_w, gn, w_out, ffn_weights):
    b, s, d = h.shape
    ts = MIXER_TOKENS
    n_seq = s // ts
    cast_in, cast_out, cast_shapes = _cast_specs(
        ffn_weights, layer, b * n_seq, lambda i, j: i * n_seq + j)
    ins = [h, ln1, w_in, w_up, b_gate, conv_w, gn, w_out]
    return pl.pallas_call(
        _with_casts(_mixer_body, len(ins), len(ffn_weights)),
        grid=(b, n_seq),
        in_specs=[
            pl.BlockSpec((1, ts, d), lambda i, j: (i, j, 0)),
            _layer_block(ln1.shape, layer),
            _resident(w_in.shape),
            _layer_block(w_up.shape, layer),
            _layer_block(b_gate.shape, layer),
            _layer_block(conv_w.shape, layer),
            _layer_block(gn.shape, layer),
            _resident(w_out.shape),
        ] + cast_in,
        out_specs=[pl.BlockSpec((1, ts, d), lambda i, j: (i, j, 0))] + cast_out,
        out_shape=[jax.ShapeDtypeStruct(h.shape, h.dtype)] + cast_shapes,
        scratch_shapes=[
            pltpu.VMEM((GLA_HEADS, GLA_DK, GLA_DV), F32),
            pltpu.VMEM((ts + SUBLANES, CONV_WIN), F32),
            pltpu.VMEM((ts, GLA_V_DIM + CONV_DIM), BF16),
            pltpu.VMEM((ts, GLA_QK_DIM), F32),
            pltpu.VMEM((ts, GLA_QK_DIM), F32),
            pltpu.VMEM((ts, GLA_QK_DIM), F32),
            pltpu.VMEM((ts, GLA_V_DIM), F32),
            pltpu.VMEM((ts, GLA_V_DIM), F32),
            pltpu.VMEM((ts, GLA_V_DIM), F32),
            pltpu.VMEM((GLA_CHUNK, GLA_V_DIM), F32),
        ],
        compiler_params=pltpu.CompilerParams(
            dimension_semantics=("arbitrary", "arbitrary"),
            vmem_limit_bytes=VMEM_LIMIT_BYTES),
        name="mixer",
    )(*ins, *[w for w, _ in ffn_weights])


def _ffn_body(h_ref, p_ref, ln2_ref, wgu_ref, wdn_ref, ln3_ref, wpg_ref, wpp_ref, lnf_ref,
              o_ref, hid_ref, *, final):
    hidden = wdn_ref.shape[0]
    tm = h_ref.shape[0]

    halves = (slice(0, tm // 2), slice(tm // 2, tm))
    o_ref[...] = _dot(p_ref[...].astype(BF16), wpp_ref[...])
    u2 = _rmsnorm(h_ref[...], ln2_ref[...]).astype(BF16)
    for c in range(hidden // FFN_COLS):
        lo = c * FFN_COLS
        a = _dot(u2, wgu_ref[:, lo:lo + FFN_COLS])
        bb = _dot(u2, wgu_ref[:, hidden + lo:hidden + lo + FFN_COLS])
        hid_ref[:, lo:lo + FFN_COLS] = (a * jax.nn.sigmoid(a) * bb).astype(BF16)
    hs = [h_ref[r, :] + _dot(hid_ref[r, :], wdn_ref[...]) for r in halves]
    gates = [jax.nn.sigmoid(_dot(_rmsnorm(hr, ln3_ref[...]).astype(BF16), wpg_ref[...]))
             for hr in hs]
    for r, hr, gate in zip(halves, hs, gates):
        hr = hr + gate * o_ref[r, :]
        if final:
            hr = _rmsnorm(hr, lnf_ref[...])
        o_ref[r, :] = hr


def _ffn(h, p, layer, ln2, w_gu, w_dn, ln3, w_pg, w_pp, lnf, next_mixer_weights, *, final):
    t, d = h.shape
    tm = FFN_TOKENS
    hidden = w_dn.shape[0]
    cast_in, cast_out, cast_shapes = _cast_specs(
        next_mixer_weights, layer + 1, t // tm, lambda i: i)
    ins = [h, p, ln2, w_gu, w_dn, ln3, w_pg, w_pp, lnf]
    return pl.pallas_call(
        _with_casts(functools.partial(_ffn_body, final=final), len(ins), len(next_mixer_weights)),
        grid=(t // tm,),
        in_specs=[
            pl.BlockSpec((tm, d), lambda i: (i, 0)),
            pl.BlockSpec((None, tm, p.shape[2]), lambda i: (layer, i, 0)),
            _layer_block(ln2.shape, layer),
            _resident(w_gu.shape),
            _resident(w_dn.shape),
            _layer_block(ln3.shape, layer),
            _resident(w_pg.shape),
            _resident(w_pp.shape),
            pl.BlockSpec(lnf.shape, lambda i: (0, 0)),
        ] + cast_in,
        out_specs=[pl.BlockSpec((tm, d), lambda i: (i, 0))] + cast_out,
        out_shape=[jax.ShapeDtypeStruct(h.shape, h.dtype)] + cast_shapes,
        scratch_shapes=[pltpu.VMEM((tm, hidden), BF16)],
        compiler_params=pltpu.CompilerParams(
            dimension_semantics=("arbitrary",),
            vmem_limit_bytes=VMEM_LIMIT_BYTES),
        name="ffn",
    )(*ins, *[w for w, _ in next_mixer_weights])


def kernel(x, p, ln1_g, w_in, w_gate_up, b_gate, conv_w, gn_g, w_out, ln2_g,
           w_gate_upffn, w_down, ln3_g, w_ple_gate, w_ple_proj, lnf_g):
    b, s, d = x.shape
    depth = w_in.shape[0]
    assert s % MIXER_TOKENS == 0 and MIXER_TOKENS % MIXER_SUB_TOKENS == 0
    assert MIXER_SUB_TOKENS % GLA_CHUNK == 0
    assert (b * s) % FFN_TOKENS == 0 and w_down.shape[1] % FFN_COLS == 0
    assert w_in.shape[2] == TAIL_OFF + GATE_RANK + CONV_WIDTH * CONV_DIM

    w_in_t = jnp.swapaxes(w_in, 1, 2)
    mixer_weights = ((w_in_t, True), (w_out, False))
    ffn_weights = tuple((w, False) for w in (w_gate_upffn, w_down, w_ple_gate, w_ple_proj))
    w_in_b, w_out_b = _cast_only(mixer_weights, 0)
    w_up_b = jnp.pad(w_gate_up.astype(BF16), ((0, 0), (0, LANES - GATE_RANK), (0, 0)))
    conv_w_p = jnp.pad(conv_w, ((0, 0), (0, 0), (GATE_RANK, CONV_WIN - CONV_DIM - GATE_RANK)))
    rowvec = lambda a: a[:, None, :]
    p_flat = p.reshape(depth, b * s, p.shape[-1])

    h = x
    for i in range(depth):
        last = i == depth - 1
        h, w_gu_b, w_dn_b, w_pg_b, w_pp_b = _mixer(
            h, i, rowvec(ln1_g), w_in_b, w_up_b, rowvec(b_gate), conv_w_p, rowvec(gn_g), w_out_b,
            ffn_weights)
        h, *next_mixer = _ffn(
            h.reshape(b * s, d), p_flat, i, rowvec(ln2_g), w_gu_b, w_dn_b, rowvec(ln3_g),
            w_pg_b, w_pp_b, lnf_g[None, :], () if last else mixer_weights, final=last)
        h = h.reshape(b, s, d)
        if not last:
            w_in_b, w_out_b = next_mixer
    return h
```

```python
import functools

import jax
import jax.numpy as jnp
from jax import lax
from jax.experimental import pallas as pl
from jax.experimental.pallas import tpu as pltpu

F32 = jnp.float32
BF16 = jnp.bfloat16

GLA_HEADS = 4
GLA_DK = 64
GLA_DV = 128
GLA_QK_DIM = GLA_HEADS * GLA_DK
GLA_V_DIM = GLA_HEADS * GLA_DV
GATE_RANK = 16
GATE_NORMALIZER = 16.0
CONV_DIM = 512
CONV_WIDTH = 3
NORM_EPS = 1e-6

LANES = 128
SUBLANES = 8
BF16_ROWS = 16
VMEM_LIMIT_BYTES = 56 * 1024 * 1024

Q_OFF = 0
K_OFF = Q_OFF + GLA_QK_DIM
V_OFF = K_OFF + GLA_QK_DIM
G_OFF = V_OFF + GLA_V_DIM
CB_OFF = G_OFF + GLA_V_DIM
CC_OFF = CB_OFF + CONV_DIM
CX_OFF = CC_OFF + CONV_DIM
GLR_OFF = CX_OFF + CONV_DIM
IN_DIM_PADDED = GLR_OFF + LANES
GLR_SRC = 2 * GLA_QK_DIM + 2 * GLA_V_DIM
W_IN_PIECES = ((0, GLR_SRC), (GLR_SRC + GATE_RANK, CONV_WIDTH * CONV_DIM), (GLR_SRC, GATE_RANK))

MIXER_TOKENS = 1024
MIXER_SUB_TOKENS = 512
GLA_CHUNK = 256
FFN_TOKENS = 1024
FFN_COLS = 256
FFN_TAIL_PARTS = 4
CAST_STEPS = 8

SAFE_DECAY_EXPONENT = 40.0


def _rmsnorm(x, g):
    ms = jnp.mean(x * x, axis=-1, keepdims=True)
    return x * lax.rsqrt(ms + NORM_EPS) * g


_dot = functools.partial(jnp.dot, preferred_element_type=F32)
_dot_nt = functools.partial(lax.dot_general, dimension_numbers=(((1,), (1,)), ((), ())),
                            preferred_element_type=F32)
_dot_tn = functools.partial(lax.dot_general, dimension_numbers=(((0,), (0,)), ((), ())),
                            preferred_element_type=F32)


def _head_dk(hd):
    return slice(hd * GLA_DK, (hd + 1) * GLA_DK)


def _head_dv(hd):
    return slice(hd * GLA_DV, (hd + 1) * GLA_DV)


def _mixer_body(h_ref, ln1_ref, win_ref, wup_ref, bg_ref, cw_ref, gn_ref, wout_ref,
                o_ref, st_ref, xs_ref, mix_ref,
                q_ref, k_ref, bc_ref, v_ref, gate_ref, stprev_ref, acc_ref):
    ts = h_ref.shape[1]
    c_len = GLA_CHUNK
    n_chunks = ts // c_len

    @pl.when(pl.program_id(1) == 0)
    def _reset_sequence_state():
        st_ref[...] = jnp.zeros_like(st_ref)
        xs_ref[0:SUBLANES, :] = jnp.zeros((SUBLANES, CONV_DIM), F32)

    def chunk_rows(c):
        return slice(c * c_len, (c + 1) * c_len)

    row = lax.broadcasted_iota(jnp.int32, (c_len, c_len), 0)
    col = lax.broadcasted_iota(jnp.int32, (c_len, c_len), 1)
    causal = row >= col
    tri = jnp.where(causal, 1.0, 0.0).astype(BF16)
    eye = (lax.broadcasted_iota(jnp.int32, (GLA_DK, GLA_DK), 0)
           == lax.broadcasted_iota(jnp.int32, (GLA_DK, GLA_DK), 1))

    def finish_heads(c, outs):
        rows = chunk_rows(c)
        for hd, o in enumerate(outs):
            dv = _head_dv(hd)
            on = _rmsnorm(o, gn_ref[...]) * gate_ref[rows, dv]
            mix_ref[rows, dv] = on.astype(BF16)

    def project_rows(c, lo, hi):
        return _dot(mix_ref[chunk_rows(c), lo:hi], wout_ref[lo:hi, :])

    def project_out(chunks, conv_parts):
        for c, conv_part in zip(chunks, conv_parts):
            rows = chunk_rows(c)
            o_ref[0, rows, :] = (h_ref[0, rows, :] + conv_part
                                 + project_rows(c, 0, GLA_V_DIM))

    def sub_tile(r0, chunks):
        sub = len(chunks) * c_len
        srows = slice(r0, r0 + sub)
        u = _rmsnorm(h_ref[0, srows, :], ln1_ref[...]).astype(BF16)

        def proj(off, width):
            return _dot(u, win_ref[:, off:off + width])

        def local(c):
            return slice(c * c_len - r0, (c + 1) * c_len - r0)

        glr = proj(GLR_OFF, LANES)
        q = proj(Q_OFF, GLA_QK_DIM) * (GLA_DK ** -0.5)
        logits = _dot(glr.astype(BF16), wup_ref[...]) + bg_ref[...]
        k = proj(K_OFF, GLA_QK_DIM)
        cb = proj(CB_OFF, CONV_DIM)
        xin = proj(CC_OFF, CONV_DIM) * proj(CX_OFF, CONV_DIM)
        log_a = ((jnp.minimum(logits, 0.0) - jnp.log1p(jnp.exp(-jnp.abs(logits))))
                 * (1.0 / GATE_NORMALIZER))

        xs_ref[SUBLANES + r0:SUBLANES + r0 + sub, :] = xin
        xe = xs_ref[r0:r0 + sub + SUBLANES, :]
        x1 = pltpu.roll(xe, 1, 0)[SUBLANES:, :]
        x2 = pltpu.roll(xe, 2, 0)[SUBLANES:, :]
        y = cb * (cw_ref[0:1, :] * x2 + cw_ref[1:2, :] * x1 + cw_ref[2:3, :] * xin)
        mix_ref[srows, GLA_V_DIM:GLA_V_DIM + CONV_DIM] = y.astype(BF16)

        la_hi = log_a.astype(BF16)
        la_lo = (log_a - la_hi.astype(F32)).astype(BF16)
        bcs = [_dot(tri, la_hi[local(c)]) + _dot(tri, la_lo[local(c)]) for c in chunks]
        v = proj(V_OFF, GLA_V_DIM)
        g = proj(G_OFF, GLA_V_DIM)
        gate_ref[srows, :] = g * jax.nn.sigmoid(g)

        q_ref[srows, :] = q
        k_ref[srows, :] = k
        v_ref[srows, :] = v
        vb = v.astype(BF16)
        qes, kes, kds, e_lasts = [], [], [], []
        for c, bc in zip(chunks, bcs):
            bc_ref[chunk_rows(c), :] = bc
            b_last = bc[c_len - 1:c_len, :]
            qes.append(q[local(c)] * jnp.exp(bc))
            kes.append(k[local(c)] * jnp.exp(-bc))
            kds.append(k[local(c)] * jnp.exp(b_last - bc))
            e_lasts.append(jnp.exp(b_last))

        scs = [[_dot_nt(qe[:, _head_dk(hd)].astype(BF16), ke[:, _head_dk(hd)].astype(BF16))
                for hd in range(GLA_HEADS)] for qe, ke in zip(qes, kes)]

        gla_outs = []
        for i, c in enumerate(chunks):
            outs = []
            for hd in range(GLA_HEADS):
                dk, dv = _head_dk(hd), _head_dv(hd)
                st = st_ref[hd]
                stprev_ref[c, hd] = st
                sc = jnp.where(causal, scs[i][hd], 0.0).astype(BF16)
                outs.append(_dot(qes[i][:, dk].astype(BF16), st.astype(BF16))
                            + _dot(sc, vb[local(c), dv]))
                e_col = jnp.sum(
                    jnp.where(eye, jnp.broadcast_to(e_lasts[i][:, dk], (GLA_DK, GLA_DK)), 0.0),
                    axis=1, keepdims=True)
                st_ref[hd] = st * e_col + _dot_tn(kds[i][:, dk].astype(BF16), vb[local(c), dv])
            gla_outs.append(outs)

        conv_parts = [project_rows(c, GLA_V_DIM, GLA_V_DIM + CONV_DIM) for c in chunks]
        for c, outs in zip(chunks, gla_outs):
            finish_heads(c, outs)
        project_out(chunks, conv_parts)
        return functools.reduce(jnp.maximum, [-bc[c_len - 1:c_len, :] for bc in bcs])

    per_sub = MIXER_SUB_TOKENS // c_len
    worst = functools.reduce(jnp.maximum, [
        sub_tile(c0 * c_len, list(range(c0, c0 + per_sub))) for c0 in range(0, n_chunks, per_sub)])
    xs_ref[0:SUBLANES, :] = xs_ref[ts:ts + SUBLANES, :]
    need_exact = jnp.max(worst) > SAFE_DECAY_EXPONENT

    @pl.when(need_exact)
    def _redo_intra_chunk_exactly():
        expand = jnp.where(
            lax.broadcasted_iota(jnp.int32, (GLA_QK_DIM, GLA_V_DIM), 0) // GLA_DK
            == lax.broadcasted_iota(jnp.int32, (GLA_QK_DIM, GLA_V_DIM), 1) // GLA_DV,
            1.0, 0.0).astype(BF16)
        row_id = lax.broadcasted_iota(jnp.int32, (c_len, 1), 0)
        for c in range(n_chunks):
            base = c * c_len
            rows = slice(base, base + c_len)
            acc_ref[...] = jnp.zeros_like(acc_ref)

            def add_source_row(s, carry):
                k_s = k_ref[pl.ds(base + s, 1), :]
                b_s = bc_ref[pl.ds(base + s, 1), :]
                v_s = v_ref[pl.ds(base + s, 1), :]
                w = q_ref[rows, :] * k_s * jnp.exp(jnp.minimum(bc_ref[rows, :] - b_s, 0.0))
                r = _dot(w.astype(BF16), expand)
                acc_ref[...] += jnp.where(row_id >= s, r, 0.0) * v_s
                return carry

            lax.fori_loop(0, c_len, add_source_row, 0)
            qe = q_ref[rows, :] * jnp.exp(bc_ref[rows, :])
            finish_heads(c, [acc_ref[:, _head_dv(hd)]
                             + _dot(qe[:, _head_dk(hd)].astype(BF16),
                                    stprev_ref[c, hd].astype(BF16))
                             for hd in range(GLA_HEADS)])
        project_out(range(n_chunks), [project_rows(c, GLA_V_DIM, GLA_V_DIM + CONV_DIM)
                                      for c in range(n_chunks)])


def _layer_block(shape, layer):
    return pl.BlockSpec((None,) + tuple(shape[1:]),
                        lambda *_: (layer,) + (0,) * (len(shape) - 1),
                        pipeline_mode=pl.Buffered(1))


def _resident(shape):
    return pl.BlockSpec(tuple(shape), lambda *_: (0,) * len(shape), pipeline_mode=pl.Buffered(1))


def _cast_rows(n_rows, n_steps):
    stride = 1
    while stride <= n_steps:
        n_blocks = n_steps // stride
        if n_steps % stride == 0 and n_rows % n_blocks == 0 and (n_rows // n_blocks) % BF16_ROWS == 0:
            return n_rows // n_blocks, stride
        stride *= 2
    raise ValueError(f"no aligned row split of {n_rows} over {n_steps} steps")


def _cast_specs(srcs, layer, n_steps, step_of):
    in_specs, out_specs, out_shapes = [], [], []
    for w, swapped, _ in srcs:
        n_rows, n_cols = (w.shape[2], w.shape[1]) if swapped else (w.shape[1], w.shape[2])
        width = -(-n_cols // LANES) * LANES
        if swapped:
            rows, stride = LANES, n_steps // (n_rows // LANES)
            assert n_rows % LANES == 0 and n_steps % (n_rows // LANES) == 0
            in_specs.append(pl.BlockSpec(
                (None, n_cols, rows), lambda *g, st=stride: (layer, 0, step_of(*g) // st)))
        else:
            rows, stride = _cast_rows(n_rows, n_steps)
            in_specs.append(pl.BlockSpec(
                (None, rows, n_cols), lambda *g, st=stride: (layer, step_of(*g) // st, 0)))
        out_specs.append(pl.BlockSpec((rows, width), lambda *g, st=stride: (step_of(*g) // st, 0)))
        out_shapes.append(jax.ShapeDtypeStruct((n_rows, width), BF16))
    return in_specs, out_specs, out_shapes


def _cast_block(src_ref, dst_ref, pieces):
    if pieces is None:
        assert src_ref.shape == dst_ref.shape
        dst_ref[...] = src_ref[...].astype(BF16)
        return
    assert src_ref.shape[1] == dst_ref.shape[0] == LANES
    assert all(c0 % SUBLANES == 0 for c0, _ in pieces)
    d = 0
    for c0, width in pieces:
        for j in range(0, width, LANES):
            n = min(LANES, width - j)
            tile = src_ref[c0 + j:c0 + j + n, :]
            if n < LANES:
                tile = jnp.pad(tile, ((0, LANES - n), (0, 0)))
            dst_ref[:, d:d + LANES] = tile.T.astype(BF16)
            d += LANES
    assert d == dst_ref.shape[1]


def _with_casts(body, n_in, cast_pieces):
    n_cast = len(cast_pieces)

    def kernel_fn(*refs):
        ins, cast_in = refs[:n_in], refs[n_in:n_in + n_cast]
        out = refs[n_in + n_cast]
        cast_out = refs[n_in + n_cast + 1:n_in + 2 * n_cast + 1]
        scratch = refs[n_in + 2 * n_cast + 1:]
        for src, dst, pieces in zip(cast_in, cast_out, cast_pieces):
            _cast_block(src, dst, pieces)
        body(*ins, out, *scratch)
    return kernel_fn


def _cast_only(srcs, layer):
    def body(*refs):
        for src, dst, (_, _, pieces) in zip(refs[:len(srcs)], refs[len(srcs):], srcs):
            _cast_block(src, dst, pieces)
    cast_in, cast_out, cast_shapes = _cast_specs(srcs, layer, CAST_STEPS, lambda i: i)
    return pl.pallas_call(
        body, grid=(CAST_STEPS,), in_specs=cast_in, out_specs=cast_out, out_shape=cast_shapes,
        compiler_params=pltpu.CompilerParams(dimension_semantics=("arbitrary",)),
        name="cast_weights",
    )(*[w for w, _, _ in srcs])


def _mixer(h, layer, ln1, w_in, w_up, b_gate, conv_w, gn, w_out, ffn_weights):
    b, s, d = h.shape
    ts = MIXER_TOKENS
    n_seq = s // ts
    cast_in, cast_out, cast_shapes = _cast_specs(
        ffn_weights, layer, b * n_seq, lambda i, j: i * n_seq + j)
    ins = [h, ln1, w_in, w_up, b_gate, conv_w, gn, w_out]
    return pl.pallas_call(
        _with_casts(_mixer_body, len(ins), [pc for _, _, pc in ffn_weights]),
        grid=(b, n_seq),
        in_specs=[
            pl.BlockSpec((1, ts, d), lambda i, j: (i, j, 0)),
            _layer_block(ln1.shape, layer),
            _resident(w_in.shape),
            _layer_block(w_up.shape, layer),
            _layer_block(b_gate.shape, layer),
            _layer_block(conv_w.shape, layer),
            _layer_block(gn.shape, layer),
            _resident(w_out.shape),
        ] + cast_in,
        out_specs=[pl.BlockSpec((1, ts, d), lambda i, j: (i, j, 0))] + cast_out,
        out_shape=[jax.ShapeDtypeStruct(h.shape, h.dtype)] + cast_shapes,
        scratch_shapes=[
            pltpu.VMEM((GLA_HEADS, GLA_DK, GLA_DV), F32),
            pltpu.VMEM((ts + SUBLANES, CONV_DIM), F32),
            pltpu.VMEM((ts, GLA_V_DIM + CONV_DIM), BF16),
            pltpu.VMEM((ts, GLA_QK_DIM), F32),
            pltpu.VMEM((ts, GLA_QK_DIM), F32),
            pltpu.VMEM((ts, GLA_QK_DIM), F32),
            pltpu.VMEM((ts, GLA_V_DIM), F32),
            pltpu.VMEM((ts, GLA_V_DIM), F32),
            pltpu.VMEM((ts // GLA_CHUNK, GLA_HEADS, GLA_DK, GLA_DV), F32),
            pltpu.VMEM((GLA_CHUNK, GLA_V_DIM), F32),
        ],
        compiler_params=pltpu.CompilerParams(
            dimension_semantics=("arbitrary", "arbitrary"),
            vmem_limit_bytes=VMEM_LIMIT_BYTES),
        name="mixer",
    )(*ins, *[w for w, _, _ in ffn_weights])


def _ffn_body(h_ref, p_ref, ln2_ref, wgu_ref, wdn_ref, ln3_ref, wpg_ref, wpp_ref, lnf_ref,
              o_ref, hid_ref, *, final):
    hidden = wdn_ref.shape[0]
    tm = h_ref.shape[0]

    halves = tuple(slice(r, r + tm // FFN_TAIL_PARTS) for r in range(0, tm, tm // FFN_TAIL_PARTS))
    o_ref[...] = _dot(p_ref[...].astype(BF16), wpp_ref[...])
    u2 = _rmsnorm(h_ref[...], ln2_ref[...]).astype(BF16)
    for c in range(hidden // FFN_COLS):
        lo = c * FFN_COLS
        a = _dot(u2, wgu_ref[:, lo:lo + FFN_COLS])
        bb = _dot(u2, wgu_ref[:, hidden + lo:hidden + lo + FFN_COLS])
        hid_ref[:, lo:lo + FFN_COLS] = (a * jax.nn.sigmoid(a) * bb).astype(BF16)
    hs = [h_ref[r, :] + _dot(hid_ref[r, :], wdn_ref[...]) for r in halves]
    gates = [jax.nn.sigmoid(_dot(_rmsnorm(hr, ln3_ref[...]).astype(BF16), wpg_ref[...]))
             for hr in hs]
    for r, hr, gate in zip(halves, hs, gates):
        hr = hr + gate * o_ref[r, :]
        if final:
            hr = _rmsnorm(hr, lnf_ref[...])
        o_ref[r, :] = hr


def _ffn(h, p, layer, ln2, w_gu, w_dn, ln3, w_pg, w_pp, lnf, next_mixer_weights, *, final):
    t, d = h.shape
    tm = FFN_TOKENS
    hidden = w_dn.shape[0]
    cast_in, cast_out, cast_shapes = _cast_specs(
        next_mixer_weights, layer + 1, t // tm, lambda i: i)
    ins = [h, p, ln2, w_gu, w_dn, ln3, w_pg, w_pp, lnf]
    return pl.pallas_call(
        _with_casts(functools.partial(_ffn_body, final=final), len(ins),
                    [pc for _, _, pc in next_mixer_weights]),
        grid=(t // tm,),
        in_specs=[
            pl.BlockSpec((tm, d), lambda i: (i, 0)),
            pl.BlockSpec((None, tm, p.shape[2]), lambda i: (layer, i, 0)),
            _layer_block(ln2.shape, layer),
            _resident(w_gu.shape),
            _resident(w_dn.shape),
            _layer_block(ln3.shape, layer),
            _resident(w_pg.shape),
            _resident(w_pp.shape),
            pl.BlockSpec(lnf.shape, lambda i: (0, 0)),
        ] + cast_in,
        out_specs=[pl.BlockSpec((tm, d), lambda i: (i, 0))] + cast_out,
        out_shape=[jax.ShapeDtypeStruct(h.shape, h.dtype)] + cast_shapes,
        scratch_shapes=[pltpu.VMEM((tm, hidden), BF16)],
        compiler_params=pltpu.CompilerParams(
            dimension_semantics=("arbitrary",),
            vmem_limit_bytes=VMEM_LIMIT_BYTES),
        name="ffn",
    )(*ins, *[w for w, _, _ in next_mixer_weights])


def kernel(x, p, ln1_g, w_in, w_gate_up, b_gate, conv_w, gn_g, w_out, ln2_g,
           w_gate_upffn, w_down, ln3_g, w_ple_gate, w_ple_proj, lnf_g):
    b, s, d = x.shape
    depth = w_in.shape[0]
    assert s % MIXER_TOKENS == 0 and MIXER_TOKENS % MIXER_SUB_TOKENS == 0
    assert MIXER_SUB_TOKENS % GLA_CHUNK == 0
    assert (b * s) % FFN_TOKENS == 0 and w_down.shape[1] % FFN_COLS == 0
    assert w_in.shape[2] == sum(width for _, width in W_IN_PIECES)

    w_in_t = jnp.swapaxes(w_in, 1, 2)
    mixer_weights = ((w_in_t, True, W_IN_PIECES), (w_out, False, None))
    ffn_weights = tuple((w, False, None)
                        for w in (w_gate_upffn, w_down, w_ple_gate, w_ple_proj))
    w_in_b, w_out_b = _cast_only(mixer_weights, 0)
    w_up_b = jnp.pad(w_gate_up.astype(BF16), ((0, 0), (0, LANES - GATE_RANK), (0, 0)))
    rowvec = lambda a: a[:, None, :]
    p_flat = p.reshape(depth, b * s, p.shape[-1])

    h = x
    for i in range(depth):
        last = i == depth - 1
        h, w_gu_b, w_dn_b, w_pg_b, w_pp_b = _mixer(
            h, i, rowvec(ln1_g), w_in_b, w_up_b, rowvec(b_gate), conv_w, rowvec(gn_g), w_out_b,
            ffn_weights)
        h, *next_mixer = _ffn(
            h.reshape(b * s, d), p_flat, i, rowvec(ln2_g), w_gu_b, w_dn_b, rowvec(ln3_g),
            w_pg_b, w_pp_b, lnf_g[None, :], () if last else mixer_weights, final=last)
        h = h.reshape(b, s, d)
        if not last:
            w_in_b, w_out_b = next_mixer
    return h
```
